```python
import math
import jax, jax.numpy as jnp
from jax import lax
import numpy as np

D_MODEL = 2048
BATCH = 2
SEQ = 4096
DEPTH = 4
DEC_BATCH = 8
DEC_SEQ = 8
PAST_LEN = 16384
PAGE_SIZE = 128

HEAD_DIM = 128
N_HEADS = D_MODEL // HEAD_DIM
D_FF = ((8 * D_MODEL // 3 + 127) // 128) * 128
N_MIXERS = 2
N_MOBA = (DEPTH + 1) // 2
N_NSA = DEPTH // 2
MOBA_BLOCK = 256
MOBA_TOPK = 3
NSA_KV_HEADS = 4
NSA_GROUP = N_HEADS // NSA_KV_HEADS
CMP_LEN = 32
CMP_STRIDE = 16
CMP_HIDDEN = 2 * HEAD_DIM
SEL_BLOCK = 64
SEL_TOPN = 16
WINDOW = 512
FORCE_BONUS = 1e4
N_BUCKETS = 32
MAX_DISTANCE = 128
QUERY_BLOCK = 32
RMS_EPS = 1e-6
ATTN_SCALE = HEAD_DIM ** -0.5

kernel_name = 'moba_nsa_macaron_hybrid_step'


def rmsnorm(x, g):
    xf = x.astype(jnp.float32)
    y = xf * lax.rsqrt(jnp.mean(xf * xf, axis=-1, keepdims=True) + RMS_EPS)
    return (y * g.astype(jnp.float32)).astype(x.dtype)


def half_ffn(x, g, w_gate, w_up, w_down):
    h = rmsnorm(x, g)
    return x + 0.5 * ((jax.nn.silu(h @ w_gate) * (h @ w_up)) @ w_down)


def pad_axis(x, axis, multiple):
    extra = (-x.shape[axis]) % multiple
    if extra == 0:
        return x
    widths = [(0, 0)] * x.ndim
    widths[axis] = (0, extra)
    return jnp.pad(x, widths)


def t5_bucket(dist):
    n = jnp.maximum(dist, 0)
    exact = N_BUCKETS // 2
    nf = jnp.maximum(n, 1).astype(jnp.float32)
    log_b = exact + (jnp.log(nf / exact) / math.log(MAX_DISTANCE / exact) * (N_BUCKETS - exact)).astype(jnp.int32)
    return jnp.where(n < exact, n, jnp.minimum(log_b, N_BUCKETS - 1))


def masked_softmax(logits, mask):
    l = jnp.where(mask, logits, -jnp.inf)
    m = jnp.max(l, axis=-1, keepdims=True)
    m = jnp.where(jnp.isfinite(m), m, 0.0)
    p = jnp.where(mask, jnp.exp(l - m), 0.0)
    return p / jnp.maximum(jnp.sum(p, axis=-1, keepdims=True), 1e-30)


def to_chunks(a):
    return jnp.moveaxis(a.reshape((a.shape[0], a.shape[1] // QUERY_BLOCK, QUERY_BLOCK) + a.shape[2:]), 1, 0)


def moba_project(h, w_qkv, g_q, g_k):
    B, T, _ = h.shape
    qkv = (h @ w_qkv).reshape(B, T, 3, N_HEADS, HEAD_DIM)
    return rmsnorm(qkv[:, :, 0], g_q), rmsnorm(qkv[:, :, 1], g_k), qkv[:, :, 2]


def moba_blocks(k, v):
    k = pad_axis(k, 1, MOBA_BLOCK)
    v = pad_axis(v, 1, MOBA_BLOCK)
    B, Tp, H, dh = k.shape
    kb = k.reshape(B, Tp // MOBA_BLOCK, MOBA_BLOCK, H, dh)
    vb = v.reshape(B, Tp // MOBA_BLOCK, MOBA_BLOCK, H, dh)
    kmean = jnp.mean(kb.astype(jnp.float32), axis=2)
    return kb, vb, kmean


def moba_attend(q, q_pos, kb, vb, kmean, rel_bias):
    B, Q, H, _ = q.shape
    nb = kb.shape[1]
    q_blk = q_pos // MOBA_BLOCK
    gate = jnp.einsum('bqhd,bnhd->bhqn', q.astype(jnp.float32), kmean)
    fully_past = jnp.arange(nb)[None, :] < q_blk[:, None]
    gate = jnp.where(fully_past, gate, -jnp.inf)
    n_top = min(MOBA_TOPK, nb)
    _, top = lax.top_k(gate, n_top)
    own = jnp.broadcast_to(q_blk[None, None, :, None], (B, H, Q, 1)).astype(top.dtype)
    idx = jnp.concatenate([top, own], axis=-1)
    k1 = n_top + 1
    bi = jnp.arange(B)[:, None, None, None]
    hi = jnp.arange(H)[None, :, None, None]
    kg = kb[bi, idx, :, hi]
    vg = vb[bi, idx, :, hi]
    kpos = idx[..., None] * MOBA_BLOCK + jnp.arange(MOBA_BLOCK)
    is_own = jnp.arange(k1) == n_top
    blk_ok = is_own | (idx < q_blk[None, None, :, None])
    mask = blk_ok[..., None] & (kpos <= q_pos[None, None, :, None, None])
    bucket = t5_bucket(q_pos[None, None, :, None, None] - kpos)
    bias = rel_bias.T[hi[..., None], bucket].astype(jnp.float32)
    logits = jnp.einsum('bqhd,bhqkld->bhqkl', q, kg).astype(jnp.float32) * ATTN_SCALE + bias
    m = k1 * MOBA_BLOCK
    p = masked_softmax(logits.reshape(B, H, Q, m), mask.reshape(B, H, Q, m))
    o = jnp.einsum('bhqm,bhqmd->bqhd', p.astype(vg.dtype), vg.reshape(B, H, Q, m, HEAD_DIM))
    return o.reshape(B, Q, H * HEAD_DIM)


def moba_prompt(h, w_qkv, w_o, g_q, g_k, rel_bias):
    B, T, _ = h.shape
    q, k, v = moba_project(h, w_qkv, g_q, g_k)
    kb, vb, kmean = moba_blocks(k, v)
    pos = jnp.arange(T, dtype=jnp.int32).reshape(T // QUERY_BLOCK, QUERY_BLOCK)
    o = lax.map(lambda c: moba_attend(c[0], c[1], kb, vb, kmean, rel_bias), (to_chunks(q), pos))
    o = jnp.moveaxis(o, 0, 1).reshape(B, T, N_HEADS * HEAD_DIM)
    return o @ w_o, jnp.stack([k, v], axis=2)


def moba_sample(h, cache_kv, layer, page_table, w_qkv, w_o, g_q, g_k, rel_bias):
    B, S, _ = h.shape
    past_len = page_table.shape[1] * cache_kv.shape[2]
    q, k, v = moba_project(h, w_qkv, g_q, g_k)
    k_past = cache_kv[layer, page_table, :, 0].reshape(B, past_len, N_HEADS, HEAD_DIM)
    v_past = cache_kv[layer, page_table, :, 1].reshape(B, past_len, N_HEADS, HEAD_DIM)
    kb, vb, kmean = moba_blocks(jnp.concatenate([k_past, k], axis=1), jnp.concatenate([v_past, v], axis=1))
    q_pos = past_len + jnp.arange(S, dtype=jnp.int32)
    o = moba_attend(q, q_pos, kb, vb, kmean, rel_bias)
    return o @ w_o, jnp.stack([k, v], axis=2)


def nsa_project(h, w_in, g_q, g_k):
    B, T, _ = h.shape
    hd = N_HEADS * HEAD_DIM
    kd = NSA_KV_HEADS * HEAD_DIM
    z = h @ w_in
    q = rmsnorm(z[..., :hd].reshape(B, T, N_HEADS, HEAD_DIM), g_q)
    kv = z[..., hd:hd + 6 * kd].reshape(B, T, 6, NSA_KV_HEADS, HEAD_DIM)
    gates = jax.nn.sigmoid(z[..., hd + 6 * kd:].astype(jnp.float32)).reshape(B, T, N_HEADS, 3).astype(h.dtype)
    rows = jnp.stack([kv[:, :, 0], kv[:, :, 1], rmsnorm(kv[:, :, 2], g_k[1]), kv[:, :, 3]], axis=2)
    win = jnp.stack([rmsnorm(kv[:, :, 4], g_k[2]), kv[:, :, 5]], axis=2)
    return q, gates, rows, win


def nsa_compress(x, pos_emb, w1, b1, w2):
    B, Tp, Hk, dh = x.shape
    r = CMP_LEN // CMP_STRIDE
    n_str = Tp // CMP_STRIDE
    nc = n_str - r + 1
    strides = x.reshape(B, n_str, CMP_STRIDE, Hk, dh)
    blocks = jnp.concatenate([strides[:, j:j + nc] for j in range(r)], axis=2)
    blocks = blocks + pos_emb[None, None, :, None, :]
    flat = jnp.swapaxes(blocks, 2, 3).reshape(B, nc, Hk, CMP_LEN * dh)
    return jax.nn.gelu(flat @ w1 + b1) @ w2


def nsa_context(rows, g_kc, cmp_pos, cmp_w1, cmp_b1, cmp_w2):
    rows = pad_axis(rows, 1, SEL_BLOCK)
    B, Tp = rows.shape[:2]
    kc = rmsnorm(nsa_compress(rows[:, :, 0], cmp_pos[0], cmp_w1[0], cmp_b1[0], cmp_w2[0]), g_kc)
    vc = nsa_compress(rows[:, :, 1], cmp_pos[1], cmp_w1[1], cmp_b1[1], cmp_w2[1])
    c_end = jnp.arange(kc.shape[1], dtype=jnp.int32) * CMP_STRIDE + CMP_LEN - 1
    ksb = rows[:, :, 2].reshape(B, Tp // SEL_BLOCK, SEL_BLOCK, NSA_KV_HEADS, HEAD_DIM)
    vsb = rows[:, :, 3].reshape(B, Tp // SEL_BLOCK, SEL_BLOCK, NSA_KV_HEADS, HEAD_DIM)
    return kc, vc, c_end, ksb, vsb


def nsa_attend(q, gates, q_pos, kc, vc, c_end, ksb, vsb, kw, vw, w_pos, rel_bias):
    B, Q, H, dh = q.shape
    Hk, G = NSA_KV_HEADS, NSA_GROUP
    qg = q.reshape(B, Q, Hk, G, dh)

    def head_bias(dist):
        b = rel_bias[t5_bucket(dist)]
        return jnp.transpose(b.reshape(dist.shape + (Hk, G)), (2, 3, 0, 1)).astype(jnp.float32)

    lc = jnp.einsum('bqkgd,bnkd->bkgqn', qg, kc).astype(jnp.float32) * ATTN_SCALE + head_bias(q_pos[:, None] - c_end[None, :])
    pc = masked_softmax(lc, c_end[None, :] <= q_pos[:, None])
    oc = jnp.einsum('bkgqn,bnkd->bqkgd', pc.astype(vc.dtype), vc)

    ns = ksb.shape[1]
    r = CMP_LEN // CMP_STRIDE
    n_str = ns * (SEL_BLOCK // CMP_STRIDE)
    pc_pad = jnp.pad(pc, [(0, 0)] * 4 + [(r - 1, r - 1)])
    stride_mass = sum(pc_pad[..., j:j + n_str] for j in range(r))
    blk_score = stride_mass.reshape(B, Hk, G, Q, ns, SEL_BLOCK // CMP_STRIDE).sum(axis=(2, 5))
    cur = q_pos // SEL_BLOCK
    jb = jnp.arange(ns)[None, :]
    forced = (jb == 0) | (jb == cur[:, None]) | (jb == cur[:, None] - 1)
    blk_score = jnp.where(jb <= cur[:, None], blk_score + jnp.where(forced, FORCE_BONUS, 0.0), -jnp.inf)
    n_sel = min(SEL_TOPN, ns)
    _, idx = lax.top_k(blk_score, n_sel)
    bi = jnp.arange(B)[:, None, None, None]
    hi = jnp.arange(Hk)[None, :, None, None]
    kg = ksb[bi, idx, :, hi]
    vg = vsb[bi, idx, :, hi]
    kpos = idx[..., None] * SEL_BLOCK + jnp.arange(SEL_BLOCK)
    mask_s = (idx <= cur[None, None, :, None])[..., None] & (kpos <= q_pos[None, None, :, None, None])
    bucket = t5_bucket(q_pos[None, None, :, None, None] - kpos)
    tab = rel_bias.T.reshape(Hk, G, N_BUCKETS)
    bias_s = tab[hi[..., None, None], jnp.arange(G)[None, None, :, None, None, None], bucket[:, :, None]].astype(jnp.float32)
    ls = jnp.einsum('bqkgd,bkqnld->bkgqnl', qg, kg).astype(jnp.float32) * ATTN_SCALE + bias_s
    m = n_sel * SEL_BLOCK
    ps = masked_softmax(ls.reshape(B, Hk, G, Q, m), mask_s.reshape(B, Hk, 1, Q, m))
    osel = jnp.einsum('bkgqm,bkqmd->bqkgd', ps.astype(vg.dtype), vg.reshape(B, Hk, Q, m, dh))

    dw = q_pos[:, None] - w_pos[None, :]
    lw = jnp.einsum('bqkgd,blkd->bkgql', qg, kw).astype(jnp.float32) * ATTN_SCALE + head_bias(dw)
    pw = masked_softmax(lw, (w_pos[None, :] >= 0) & (dw >= 0) & (dw < WINDOW))
    ow = jnp.einsum('bkgql,blkd->bqkgd', pw.astype(vw.dtype), vw)

    g = gates.reshape(B, Q, Hk, G, 3, 1)
    o = g[..., 0, :] * oc + g[..., 1, :] * osel + g[..., 2, :] * ow
    return o.reshape(B, Q, H * dh)


def nsa_prompt(h, w_in, w_o, g_q, g_k, cmp_pos, cmp_w1, cmp_b1, cmp_w2, rel_bias):
    B, T, _ = h.shape
    q, gates, rows, win = nsa_project(h, w_in, g_q, g_k)
    kc, vc, c_end, ksb, vsb = nsa_context(rows, g_k[0], cmp_pos, cmp_w1, cmp_b1, cmp_w2)
    win_pad = jnp.pad(win, ((0, 0), (WINDOW, 0), (0, 0), (0, 0), (0, 0)))

    def chunk(c):
        qc, gc, pos = c
        start = pos[0]
        wrows = lax.dynamic_slice_in_dim(win_pad, start, WINDOW + QUERY_BLOCK, axis=1)
        w_pos = start - WINDOW + jnp.arange(WINDOW + QUERY_BLOCK, dtype=jnp.int32)
        return nsa_attend(qc, gc, pos, kc, vc, c_end, ksb, vsb, wrows[:, :, 0], wrows[:, :, 1], w_pos, rel_bias)

    pos = jnp.arange(T, dtype=jnp.int32).reshape(T // QUERY_BLOCK, QUERY_BLOCK)
    o = lax.map(chunk, (to_chunks(q), to_chunks(gates), pos))
    o = jnp.moveaxis(o, 0, 1).reshape(B, T, N_HEADS * HEAD_DIM)
    keep = min(WINDOW, T)
    return o @ w_o, rows, win[:, T - keep:]


def nsa_sample(h, cache_kv, win_buf, layer, page_table, w_in, w_o, g_q, g_k, cmp_pos, cmp_w1, cmp_b1, cmp_w2, rel_bias):
    B, S, _ = h.shape
    past_len = page_table.shape[1] * cache_kv.shape[2]
    q, gates, rows, win = nsa_project(h, w_in, g_q, g_k)
    past = cache_kv[layer, page_table].reshape((B, past_len) + cache_kv.shape[3:])
    kc, vc, c_end, ksb, vsb = nsa_context(jnp.concatenate([past, rows], axis=1), g_k[0], cmp_pos, cmp_w1, cmp_b1, cmp_w2)
    w_all = jnp.concatenate([win_buf, win], axis=1)
    wb = win_buf.shape[1]
    w_pos = past_len - wb + jnp.arange(wb + S, dtype=jnp.int32)
    q_pos = past_len + jnp.arange(S, dtype=jnp.int32)
    o = nsa_attend(q, gates, q_pos, kc, vc, c_end, ksb, vsb, w_all[:, :, 0], w_all[:, :, 1], w_pos, rel_bias)
    return o @ w_o, rows, w_all[:, S:]


def setup_inputs(seed: int = 0) -> dict:
    key = jax.random.key(seed)
    ks = jax.random.split(key, 24)
    hd = N_HEADS * HEAD_DIM
    kd = NSA_KV_HEADS * HEAD_DIM
    n_pages = PAST_LEN // PAGE_SIZE
    n_used = DEC_BATCH * n_pages
    n_pool = n_used + max(1, n_used // 4)
    wb = min(WINDOW, PAST_LEN)

    def nrm(k, shape, scale=1.0):
        return jax.random.normal(k, shape, jnp.float32) * scale

    def gain(k, shape):
        return 1.0 + nrm(k, shape, 0.02)

    page_table = jax.random.permutation(ks[5], n_pool)[:n_used].reshape(DEC_BATCH, n_pages).astype(jnp.int32)
    return {
        'x_prompt': nrm(ks[0], (BATCH, SEQ, D_MODEL)),
        'x_sample': nrm(ks[1], (DEC_BATCH, DEC_SEQ, D_MODEL)),
        'cache_moba_kv': nrm(ks[2], (N_MOBA, n_pool, PAGE_SIZE, 2, N_HEADS, HEAD_DIM)),
        'cache_nsa_kv': nrm(ks[3], (N_NSA, n_pool, PAGE_SIZE, 4, NSA_KV_HEADS, HEAD_DIM)),
        'state_nsa_win': nrm(ks[4], (N_NSA, DEC_BATCH, wb, 2, NSA_KV_HEADS, HEAD_DIM)),
        'page_table': page_table,
        'rel_bias': nrm(ks[6], (N_BUCKETS, N_HEADS), 0.5),
        'ffn_norm': gain(ks[7], (DEPTH, 2, D_MODEL)),
        'ffn_w_gate': nrm(ks[8], (DEPTH, 2, D_MODEL, D_FF), D_MODEL ** -0.5),
        'ffn_w_up': nrm(ks[9], (DEPTH, 2, D_MODEL, D_FF), D_MODEL ** -0.5),
        'ffn_w_down': nrm(ks[10], (DEPTH, 2, D_FF, D_MODEL), D_FF ** -0.5),
        'mix_norm': gain(ks[11], (DEPTH, D_MODEL)),
        'moba_w_qkv': nrm(ks[12], (N_MOBA, D_MODEL, 3 * hd), D_MODEL ** -0.5),
        'moba_w_o': nrm(ks[13], (N_MOBA, hd, D_MODEL), hd ** -0.5),
        'moba_q_norm': gain(ks[14], (N_MOBA, HEAD_DIM)),
        'moba_k_norm': gain(ks[15], (N_MOBA, HEAD_DIM)),
        'nsa_w_in': nrm(ks[16], (N_NSA, D_MODEL, hd + 6 * kd + 3 * N_HEADS), D_MODEL ** -0.5),
        'nsa_w_o': nrm(ks[17], (N_NSA, hd, D_MODEL), hd ** -0.5),
        'nsa_q_norm': gain(ks[18], (N_NSA, HEAD_DIM)),
        'nsa_k_norm': gain(ks[19], (N_NSA, 3, HEAD_DIM)),
        'nsa_cmp_pos': nrm(ks[20], (N_NSA, 2, CMP_LEN, HEAD_DIM), 0.1),
        'nsa_cmp_w1': nrm(ks[21], (N_NSA, 2, CMP_LEN * HEAD_DIM, CMP_HIDDEN), (CMP_LEN * HEAD_DIM) ** -0.5),
        'nsa_cmp_b1': nrm(ks[22], (N_NSA, 2, CMP_HIDDEN), 0.01),
        'nsa_cmp_w2': nrm(ks[23], (N_NSA, 2, CMP_HIDDEN, HEAD_DIM), CMP_HIDDEN ** -0.5),
    }


def reference(x_prompt, x_sample, cache_moba_kv, cache_nsa_kv, state_nsa_win, page_table, rel_bias,
              ffn_norm, ffn_w_gate, ffn_w_up, ffn_w_down, mix_norm,
              moba_w_qkv, moba_w_o, moba_q_norm, moba_k_norm,
              nsa_w_in, nsa_w_o, nsa_q_norm, nsa_k_norm, nsa_cmp_pos, nsa_cmp_w1, nsa_cmp_b1, nsa_cmp_w2):
    xp, xs = x_prompt, x_sample
    moba_p, moba_s, nsa_p, nsa_s, win_p, win_s = [], [], [], [], [], []
    for i in range(DEPTH):
        j = i // N_MIXERS
        xp = half_ffn(xp, ffn_norm[i, 0], ffn_w_gate[i, 0], ffn_w_up[i, 0], ffn_w_down[i, 0])
        xs = half_ffn(xs, ffn_norm[i, 0], ffn_w_gate[i, 0], ffn_w_up[i, 0], ffn_w_down[i, 0])
        hp = rmsnorm(xp, mix_norm[i])
        hs = rmsnorm(xs, mix_norm[i])
        if i % N_MIXERS == 0:
            op, kv_p = moba_prompt(hp, moba_w_qkv[j], moba_w_o[j], moba_q_norm[j], moba_k_norm[j], rel_bias)
            osm, kv_s = moba_sample(hs, cache_moba_kv, j, page_table, moba_w_qkv[j], moba_w_o[j],
                                    moba_q_norm[j], moba_k_norm[j], rel_bias)
            moba_p.append(kv_p)
            moba_s.append(kv_s)
        else:
            op, rows_p, wp = nsa_prompt(hp, nsa_w_in[j], nsa_w_o[j], nsa_q_norm[j], nsa_k_norm[j],
                                        nsa_cmp_pos[j], nsa_cmp_w1[j], nsa_cmp_b1[j], nsa_cmp_w2[j], rel_bias)
            osm, rows_s, ws = nsa_sample(hs, cache_nsa_kv, state_nsa_win[j], j, page_table, nsa_w_in[j], nsa_w_o[j],
                                         nsa_q_norm[j], nsa_k_norm[j], nsa_cmp_pos[j], nsa_cmp_w1[j],
                                         nsa_cmp_b1[j], nsa_cmp_w2[j], rel_bias)
            nsa_p.append(rows_p)
            nsa_s.append(rows_s)
            win_p.append(wp)
            win_s.append(ws)
        xp = xp + op
        xs = xs + osm
        xp = half_ffn(xp, ffn_norm[i, 1], ffn_w_gate[i, 1], ffn_w_up[i, 1], ffn_w_down[i, 1])
        xs = half_ffn(xs, ffn_norm[i, 1], ffn_w_gate[i, 1], ffn_w_up[i, 1], ffn_w_down[i, 1])
    return (xp, xs, jnp.stack(moba_p), jnp.stack(moba_s), jnp.stack(nsa_p), jnp.stack(nsa_s), jnp.stack(win_p), jnp.stack(win_s))
```

```python
import functools
import math

import jax
import jax.numpy as jnp
from jax import lax
from jax.experimental import pallas as pl
from jax.experimental.pallas import tpu as pltpu

F32 = jnp.float32
BF16 = jnp.bfloat16
I32 = jnp.int32
HIGHEST = lax.Precision.HIGHEST

D_MODEL = 2048
DEPTH = 4
HEAD_DIM = 128
N_HEADS = 16
HD = N_HEADS * HEAD_DIM
D_FF = 5504
D_FF_PAD = 5632
MOBA_BLOCK = 256
MOBA_TOPK = 3
NSA_KV_HEADS = 4
NSA_GROUP = 4
KD = NSA_KV_HEADS * HEAD_DIM
CMP_LEN = 32
CMP_STRIDE = 16
CMP_HIDDEN = 256
SEL_BLOCK = 64
SEL_TOPN = 16
WINDOW = 512
FORCE_BONUS = 1e4
N_BUCKETS = 32
MAX_DISTANCE = 128
RMS_EPS = 1e-6
ATTN_SCALE = HEAD_DIM ** -0.5
PAGE = 128
TQ = 256
NSA_IN = HD + 6 * KD + 3 * N_HEADS
NSA_IN_PAD = 5376
NEG = -1e30
VMEM_LIMIT = 56 * 1024 * 1024

NT = (((1,), (1,)), ((), ()))


def _params(sem, vmem=VMEM_LIMIT):
    return pltpu.CompilerParams(dimension_semantics=sem, vmem_limit_bytes=vmem)


def _dot(a, b, **kw):
    return jnp.dot(a, b, preferred_element_type=F32, **kw)


def _dot_nt(a, b, **kw):
    return lax.dot_general(a, b, NT, preferred_element_type=F32, **kw)


def _iota(shape, dim):
    return lax.broadcasted_iota(I32, shape, dim)


def _rms_scale(x):
    return lax.rsqrt(jnp.mean(x * x, axis=-1, keepdims=True) + RMS_EPS)


def _t5_bucket(dist):
    n = jnp.maximum(dist, 0)
    exact = N_BUCKETS // 2
    nf = jnp.maximum(n, 1).astype(F32)
    log_b = exact + (jnp.log(nf * (1.0 / exact)) / math.log(MAX_DISTANCE / exact) * (N_BUCKETS - exact)).astype(I32)
    return jnp.where(n < exact, n, jnp.minimum(log_b, N_BUCKETS - 1))


def _bias_lookup(bucket, tab_ref, h):
    acc = jnp.zeros(bucket.shape, F32)
    for b in range(N_BUCKETS):
        acc = jnp.where(bucket == b, tab_ref[b, h], acc)
    return acc


def _topk_mask(score, idx, k):
    rank = jnp.zeros(score.shape, I32)
    for m in range(score.shape[1]):
        c = score[:, m:m + 1]
        beats = (c > score) | ((c == score) & (idx > m))
        rank = rank + beats.astype(I32)
    return rank < k


def _pick_col(mat, idx, j):
    return jnp.sum(jnp.where(idx == j, mat, 0.0), axis=1, keepdims=True)


def _softmax_first(s, mask):
    s = jnp.where(mask, s, NEG)
    m = jnp.max(s, axis=1, keepdims=True)
    p = jnp.where(mask, jnp.exp(s - m), 0.0)
    return m, jnp.sum(p, axis=1, keepdims=True), p


def _softmax_next(s, mask, m, l):
    s = jnp.where(mask, s, NEG)
    m_new = jnp.maximum(m, jnp.max(s, axis=1, keepdims=True))
    a = jnp.exp(m - m_new)
    p = jnp.where(mask, jnp.exp(s - m_new), 0.0)
    return m_new, a * l + jnp.sum(p, axis=1, keepdims=True), p, a


def _ffn_kernel(x_ref, g_ref, wg_ref, wu_ref, wd_ref, o_ref, h_ref, acc_ref):
    j = pl.program_id(1)

    @pl.when(j == 0)
    def _():
        x = x_ref[...]
        h_ref[...] = (x * _rms_scale(x) * g_ref[...]).astype(BF16)
        acc_ref[...] = jnp.zeros_like(acc_ref)

    h = h_ref[...]
    a = _dot(h, wg_ref[...])
    b = _dot(h, wu_ref[...])
    t = (a * jax.nn.sigmoid(a) * b).astype(BF16)
    acc_ref[...] += _dot(t, wd_ref[...])

    @pl.when(j == pl.num_programs(1) - 1)
    def _():
        o_ref[...] = x_ref[...] + 0.5 * acc_ref[...]


def _ffn(x, g, wg, wu, wd, tm, tf=512):
    n = x.shape[0]
    fp = wg.shape[1]
    return pl.pallas_call(
        _ffn_kernel,
        out_shape=jax.ShapeDtypeStruct((n, D_MODEL), F32),
        grid=(n // tm, fp // tf),
        in_specs=[
            pl.BlockSpec((tm, D_MODEL), lambda i, j: (i, 0)),
            pl.BlockSpec((1, D_MODEL), lambda i, j: (0, 0)),
            pl.BlockSpec((D_MODEL, tf), lambda i, j: (0, j)),
            pl.BlockSpec((D_MODEL, tf), lambda i, j: (0, j)),
            pl.BlockSpec((tf, D_MODEL), lambda i, j: (j, 0)),
        ],
        out_specs=pl.BlockSpec((tm, D_MODEL), lambda i, j: (i, 0)),
        scratch_shapes=[pltpu.VMEM((tm, D_MODEL), BF16), pltpu.VMEM((tm, D_MODEL), F32)],
        compiler_params=_params(("parallel", "arbitrary")),
        name="ffn",
    )(x, g.reshape(1, D_MODEL), wg, wu, wd)


def _proj_kernel(x_ref, g_ref, w_ref, f_ref, gn_ref, o_ref, h_ref):
    @pl.when(pl.program_id(1) == 0)
    def _():
        x = x_ref[...]
        h_ref[...] = (x * _rms_scale(x) * g_ref[...]).astype(BF16)

    y = _dot(h_ref[...], w_ref[...])
    f = f_ref[...]
    gn = gn_ref[...]
    for c in range(y.shape[1] // HEAD_DIM):
        sl = slice(c * HEAD_DIM, (c + 1) * HEAD_DIM)
        yc = y[:, sl]
        sc = f[:, sl] * _rms_scale(yc) + (1.0 - f[:, sl])
        o_ref[:, sl] = yc * sc * gn[:, sl]


def _proj(x, g, w, flag, gain, tm, tn):
    n = x.shape[0]
    npad = w.shape[1]
    return pl.pallas_call(
        _proj_kernel,
        out_shape=jax.ShapeDtypeStruct((n, npad), F32),
        grid=(n // tm, npad // tn),
        in_specs=[
            pl.BlockSpec((tm, D_MODEL), lambda i, j: (i, 0)),
            pl.BlockSpec((1, D_MODEL), lambda i, j: (0, 0)),
            pl.BlockSpec((D_MODEL, tn), lambda i, j: (0, j)),
            pl.BlockSpec((1, tn), lambda i, j: (0, j)),
            pl.BlockSpec((1, tn), lambda i, j: (0, j)),
        ],
        out_specs=pl.BlockSpec((tm, tn), lambda i, j: (i, j)),
        scratch_shapes=[pltpu.VMEM((tm, D_MODEL), BF16)],
        compiler_params=_params(("parallel", "arbitrary")),
        name="proj",
    )(x, g.reshape(1, D_MODEL), w, flag, gain)


def _oproj_kernel(a_ref, w_ref, x_ref, o_ref):
    o_ref[...] = x_ref[...] + _dot(a_ref[...].astype(BF16), w_ref[...])


def _oproj(a, w, x, tm, tn=512):
    n = a.shape[0]
    return pl.pallas_call(
        _oproj_kernel,
        out_shape=jax.ShapeDtypeStruct((n, D_MODEL), F32),
        grid=(n // tm, D_MODEL // tn),
        in_specs=[
            pl.BlockSpec((tm, HD), lambda i, j: (i, 0)),
            pl.BlockSpec((HD, tn), lambda i, j: (0, j)),
            pl.BlockSpec((tm, tn), lambda i, j: (i, j)),
        ],
        out_specs=pl.BlockSpec((tm, tn), lambda i, j: (i, j)),
        compiler_params=_params(("parallel", "arbitrary")),
        name="oproj",
    )(a, w, x)


def _bias_kernel(tab_ref, o_ref, *, base, stride, cmul):
    h = pl.program_id(0)
    nk, rows, cols = o_ref.shape
    d0 = _iota((rows, cols), 0) - cmul * _iota((rows, cols), 1) + base
    for kind in range(nk):
        o_ref[kind] = _bias_lookup(_t5_bucket(d0 + kind * stride), tab_ref, h)


def _bias_tiles(rel_bias, nk, rows, cols, base, stride, cmul=1):
    return pl.pallas_call(
        functools.partial(_bias_kernel, base=base, stride=stride, cmul=cmul),
        out_shape=jax.ShapeDtypeStruct((nk, N_HEADS, rows, cols), F32),
        grid=(N_HEADS,),
        in_specs=[pl.BlockSpec(memory_space=pltpu.SMEM)],
        out_specs=pl.BlockSpec((nk, None, rows, cols), lambda h: (0, h, 0, 0)),
        compiler_params=_params(("arbitrary",)),
        name="bias_tiles",
    )(rel_bias)


def _kmean_kernel(pt_ref, k0_ref, k1_ref, o_ref):
    s = jnp.sum(k0_ref[...], axis=0, keepdims=True) + jnp.sum(k1_ref[...], axis=0, keepdims=True)
    o_ref[...] = s * (1.0 / MOBA_BLOCK)


def _kmean(pages, layer, kcol, page_table):
    nb_, npg = page_table.shape
    nblk = npg // 2
    spec = lambda e: pl.BlockSpec((None, None, PAGE, HD), lambda b, n, pt: (layer, pt[b, 2 * n + e], 0, kcol))
    out = pl.pallas_call(
        _kmean_kernel,
        out_shape=jax.ShapeDtypeStruct((nb_, nblk, 1, HD), F32),
        grid_spec=pltpu.PrefetchScalarGridSpec(
            num_scalar_prefetch=1, grid=(nb_, nblk),
            in_specs=[spec(0), spec(1)],
            out_specs=pl.BlockSpec((None, None, 1, HD), lambda b, n, pt: (b, n, 0, 0))),
        compiler_params=_params(("parallel", "arbitrary")),
        name="kmean",
    )(page_table, pages, pages)
    return out.reshape(nb_, nblk, HD)


def _moba_prompt_kernel(q_ref, k_ref, v_ref, km_ref, bias_ref, o_ref):
    i = pl.program_id(2)
    q = q_ref[...]
    gate = _dot_nt(q, km_ref[...], precision=HIGHEST)
    nidx = _iota(gate.shape, 1)
    past = nidx < i
    gate = jnp.where(past, gate, -jnp.inf)
    sel = (_topk_mask(gate, nidx, MOBA_TOPK) & past).astype(F32)
    qb = q.astype(BF16)
    row = _iota((TQ, TQ), 0)
    col = _iota((TQ, TQ), 1)

    def tile(j, kind):
        sl = pl.ds(pl.multiple_of(j * TQ, TQ), TQ)
        s = _dot_nt(qb, k_ref[sl, :].astype(BF16)) * ATTN_SCALE + bias_ref[kind]
        return s, v_ref[sl, :].astype(BF16)

    s, vb = tile(i, 0)
    s = jnp.where(col <= row, s, NEG)
    m = jnp.max(s, axis=1, keepdims=True)
    p = jnp.exp(s - m)
    l = jnp.sum(p, axis=1, keepdims=True)
    acc = _dot(p.astype(BF16), vb)

    def body(j, carry):
        m, l, acc = carry
        s, vb = tile(j, jnp.minimum(i - j, 2))
        s = jnp.where(_pick_col(sel, nidx, j) > 0.5, s, NEG)
        m_new = jnp.maximum(m, jnp.max(s, axis=1, keepdims=True))
        a = jnp.exp(m - m_new)
        p = jnp.exp(s - m_new)
        return m_new, a * l + jnp.sum(p, axis=1, keepdims=True), a * acc + _dot(p.astype(BF16), vb)

    m, l, acc = lax.fori_loop(0, i, body, (m, l, acc))
    o_ref[...] = acc / l


def _moba_prompt_attn(y, kmean, bias, nbatch, t):
    nq = t // TQ
    return pl.pallas_call(
        _moba_prompt_kernel,
        out_shape=jax.ShapeDtypeStruct((nbatch * t, HD), F32),
        grid=(nbatch, N_HEADS, nq),
        in_specs=[
            pl.BlockSpec((TQ, HEAD_DIM), lambda b, h, i: (b * nq + i, h)),
            pl.BlockSpec((t, HEAD_DIM), lambda b, h, i: (b, N_HEADS + h)),
            pl.BlockSpec((t, HEAD_DIM), lambda b, h, i: (b, 2 * N_HEADS + h)),
            pl.BlockSpec((None, nq, HEAD_DIM), lambda b, h, i: (b, 0, h)),
            pl.BlockSpec((3, None, TQ, TQ), lambda b, h, i: (0, h, 0, 0)),
        ],
        out_specs=pl.BlockSpec((TQ, HEAD_DIM), lambda b, h, i: (b * nq + i, h)),
        compiler_params=_params(("parallel", "parallel", "arbitrary")),
        name="moba_prompt_attn",
    )(y, y, y, kmean, bias)


def _moba_sample_kernel(pt_ref, q_ref, km_ref, bias_ref, knew_ref, vnew_ref, kpg_ref, vpg_ref, o_ref,
                        qb_ref, sel_ref, m_ref, l_ref, acc_ref, *, n_pages, s_len):
    p = pl.program_id(1)
    nblk = km_ref.shape[0]
    rows = N_HEADS * s_len
    nidx = _iota((rows, nblk), 1)

    def update(first, k, v, kind, mask):
        s = _dot_nt(qb_ref[...], k.astype(BF16)) * ATTN_SCALE + bias_ref[kind]
        if first:
            m, l, pr = _softmax_first(s, mask)
            acc_ref[...] = _dot(pr.astype(BF16), v.astype(BF16))
        else:
            m, l, pr, a = _softmax_next(s, mask, m_ref[...], l_ref[...])
            acc_ref[...] = a * acc_ref[...] + _dot(pr.astype(BF16), v.astype(BF16))
        m_ref[...] = m
        l_ref[...] = l

    @pl.when(p == 0)
    def _():
        q = q_ref[...]
        blocks = []
        for h in range(N_HEADS):
            sl = slice(h * HEAD_DIM, (h + 1) * HEAD_DIM)
            lane_ok = (_iota((s_len, HD), 1) // HEAD_DIM) == h
            blocks.append(jnp.where(lane_ok, jnp.concatenate([q[:, sl]] * N_HEADS, axis=1), 0.0))
        qbd = jnp.concatenate(blocks, axis=0)
        gate = _dot_nt(qbd, km_ref[...], precision=HIGHEST)
        sel_ref[...] = _topk_mask(gate, nidx, MOBA_TOPK).astype(F32)
        qb_ref[...] = qbd.astype(BF16)
        t = _iota((rows, PAGE), 0) % s_len
        update(True, knew_ref[...], vnew_ref[...], 0, _iota((rows, PAGE), 1) <= t)

    @pl.when(p > 0)
    def _():
        pg = p - 1
        picked = _pick_col(sel_ref[...], nidx, pg // 2) > 0.5
        mask = jnp.broadcast_to(picked, (rows, PAGE))
        update(False, kpg_ref[...], vpg_ref[...], jnp.minimum(n_pages - pg, 2), mask)

    @pl.when(p == n_pages)
    def _():
        inv = 1.0 / l_ref[...]
        for h in range(N_HEADS):
            sl = slice(h * HEAD_DIM, (h + 1) * HEAD_DIM)
            rs = slice(h * s_len, (h + 1) * s_len)
            o_ref[:, sl] = acc_ref[rs, sl] * inv[rs]


def _moba_sample_attn(y, kmean, bias, kv_new, cache, layer, page_table, s_len):
    nb_, n_pages = page_table.shape
    nblk = n_pages // 2
    rows = N_HEADS * s_len
    cache5 = cache.reshape(cache.shape[0], cache.shape[1], PAGE, 2 * HD)
    pg_spec = lambda e: pl.BlockSpec(
        (None, None, PAGE, HD), lambda b, p, pt: (layer, pt[b, jnp.maximum(p - 1, 0)], 0, e))
    return pl.pallas_call(
        functools.partial(_moba_sample_kernel, n_pages=n_pages, s_len=s_len),
        out_shape=jax.ShapeDtypeStruct((nb_ * s_len, HD), F32),
        grid_spec=pltpu.PrefetchScalarGridSpec(
            num_scalar_prefetch=1, grid=(nb_, n_pages + 1),
            in_specs=[
                pl.BlockSpec((s_len, HD), lambda b, p, pt: (b, 0)),
                pl.BlockSpec((None, nblk, HD), lambda b, p, pt: (b, 0, 0)),
                pl.BlockSpec((3, rows, PAGE), lambda b, p, pt: (0, 0, 0)),
                pl.BlockSpec((None, PAGE, HD), lambda b, p, pt: (b, 0, 0)),
                pl.BlockSpec((None, PAGE, HD), lambda b, p, pt: (b, 0, 1)),
                pg_spec(0), pg_spec(1),
            ],
            out_specs=pl.BlockSpec((s_len, HD), lambda b, p, pt: (b, 0)),
            scratch_shapes=[
                pltpu.VMEM((rows, HD), BF16), pltpu.VMEM((rows, nblk), F32),
                pltpu.VMEM((rows, 1), F32), pltpu.VMEM((rows, 1), F32), pltpu.VMEM((rows, HD), F32)]),
        compiler_params=_params(("parallel", "arbitrary")),
        name="moba_sample_attn",
    )(page_table, y, kmean, bias, kv_new, kv_new, cache5, cache5)


def _cmp_a_kernel(pt_ref, *refs, pg):
    x_refs, w_ref, o_ref, lhs_ref, slab_ref = refs[:pg], refs[pg], refs[pg + 1], refs[pg + 2], refs[pg + 3]
    n_str = PAGE // CMP_STRIDE
    for kind in range(2):
        for c in range(pg):
            for k in range(NSA_KV_HEADS):
                r0 = (k * pg + c) * n_str
                col = (kind * NSA_KV_HEADS + k) * HEAD_DIM
                slab = slab_ref.at[(kind * pg + c) * NSA_KV_HEADS + k]
                slab[...] = x_refs[c][:, col:col + HEAD_DIM]
                for l in range(CMP_STRIDE):
                    lhs_ref[kind, r0:r0 + n_str, l * HEAD_DIM:(l + 1) * HEAD_DIM] = (
                        slab[pl.ds(l, n_str, stride=CMP_STRIDE), :])
        res = _dot(lhs_ref[kind].astype(BF16), w_ref[kind])
        for k in range(NSA_KV_HEADS):
            o_ref[kind, k] = res[k * pg * n_str:(k + 1) * pg * n_str]


def _cmp_a(pages, layer, ccol, page_table, wcat, pg=8):
    nb_, npg = page_table.shape
    n_str = PAGE // CMP_STRIDE
    spec = lambda c: pl.BlockSpec((None, None, PAGE, 2 * KD), lambda b, s, pt: (layer, pt[b, s * pg + c], 0, ccol))
    return pl.pallas_call(
        functools.partial(_cmp_a_kernel, pg=pg),
        out_shape=jax.ShapeDtypeStruct((nb_, 2, NSA_KV_HEADS, npg * n_str, 2 * CMP_HIDDEN), F32),
        grid_spec=pltpu.PrefetchScalarGridSpec(
            num_scalar_prefetch=1, grid=(nb_, npg // pg),
            in_specs=[spec(c) for c in range(pg)] + [
                pl.BlockSpec((2, CMP_STRIDE * HEAD_DIM, 2 * CMP_HIDDEN), lambda b, s, pt: (0, 0, 0))],
            out_specs=pl.BlockSpec((None, 2, NSA_KV_HEADS, pg * n_str, 2 * CMP_HIDDEN),
                                   lambda b, s, pt: (b, 0, 0, s, 0)),
            scratch_shapes=[pltpu.VMEM((2, NSA_KV_HEADS * pg * n_str, CMP_STRIDE * HEAD_DIM), F32),
                            pltpu.VMEM((2 * pg * NSA_KV_HEADS, PAGE, HEAD_DIM), F32)]),
        compiler_params=_params(("parallel", "arbitrary")),
        name="cmp_a",
    )(page_table, *([pages] * pg), wcat)


def _c0_kernel(p_ref, w_ref, b_ref, o_ref):
    o_ref[...] = _dot(p_ref[...], w_ref[...]) + b_ref[...]


def _cmp_c0(pos_flat, w1, b1):
    return pl.pallas_call(
        _c0_kernel,
        out_shape=jax.ShapeDtypeStruct((2, 8, CMP_HIDDEN), F32),
        grid=(2,),
        in_specs=[
            pl.BlockSpec((None, 8, CMP_LEN * HEAD_DIM), lambda i: (i, 0, 0)),
            pl.BlockSpec((None, CMP_LEN * HEAD_DIM, CMP_HIDDEN), lambda i: (i, 0, 0)),
            pl.BlockSpec((None, 1, CMP_HIDDEN), lambda i: (i, 0, 0)),
        ],
        out_specs=pl.BlockSpec((None, 8, CMP_HIDDEN), lambda i: (i, 0, 0)),
        compiler_params=_params(("arbitrary",)),
        name="cmp_c0",
    )(pos_flat, w1, b1)


def _cmp_b_kernel(a_ref, c0_ref, w2_ref, f_ref, g_ref, o_ref):
    n = a_ref.shape[0]
    a0 = a_ref[:, 0:CMP_HIDDEN]
    a1 = pltpu.roll(a_ref[:, CMP_HIDDEN:2 * CMP_HIDDEN], n - 1, 0)
    h1 = a0 + a1 + c0_ref[0:1, :]
    y = _dot(jax.nn.gelu(h1).astype(BF16), w2_ref[...])
    f = f_ref[...]
    o_ref[...] = y * (f * _rms_scale(y) + (1.0 - f)) * g_ref[...]


def _cmp_b(a, c0, w2, flag, gain):
    nb_, _, _, n_str, _ = a.shape
    return pl.pallas_call(
        _cmp_b_kernel,
        out_shape=jax.ShapeDtypeStruct((nb_, 2, n_str, KD), F32),
        grid=(nb_, 2, NSA_KV_HEADS),
        in_specs=[
            pl.BlockSpec((None, None, None, n_str, 2 * CMP_HIDDEN), lambda b, c, k: (b, c, k, 0, 0)),
            pl.BlockSpec((None, 8, CMP_HIDDEN), lambda b, c, k: (c, 0, 0)),
            pl.BlockSpec((None, CMP_HIDDEN, HEAD_DIM), lambda b, c, k: (c, 0, 0)),
            pl.BlockSpec((None, 1, HEAD_DIM), lambda b, c, k: (c, 0, 0)),
            pl.BlockSpec((None, 1, HEAD_DIM), lambda b, c, k: (c, 0, 0)),
        ],
        out_specs=pl.BlockSpec((None, None, n_str, HEAD_DIM), lambda b, c, k: (b, c, 0, k)),
        compiler_params=_params(("parallel", "arbitrary", "arbitrary")),
        name="cmp_b",
    )(a, c0, w2, flag, gain)


def _stride_to_block_matrix(n_cmp, n_blk, transpose):
    shape = (n_blk, n_cmp) if transpose else (n_cmp, n_blk)
    n = _iota(shape, 1 if transpose else 0)
    j = _iota(shape, 0 if transpose else 1)
    r = n - (SEL_BLOCK // CMP_STRIDE) * j
    return jnp.where((r == -1) | (r == 3), 1.0, jnp.where((r >= 0) & (r <= 2), 2.0, 0.0)).astype(F32)


def _nsa_prompt_kernel(tab_ref, q_ref, g_ref, kc_ref, vc_ref, ks_ref, vs_ref, kw_ref, vw_ref, bias_ref, o_ref):
    kh = pl.program_id(1)
    i = pl.program_id(2)
    q0 = i * TQ
    G = NSA_GROUP
    n_cmp = kc_ref.shape[0]
    n_blk = ks_ref.shape[0] // SEL_BLOCK
    qg = jnp.concatenate([q_ref[:, g * HEAD_DIM:(g + 1) * HEAD_DIM] for g in range(G)], axis=0).astype(BF16)

    tq = q0 + _iota((TQ, n_cmp), 0)
    dist = tq - (CMP_STRIDE * _iota((TQ, n_cmp), 1) + CMP_LEN - 1)
    bucket = _t5_bucket(dist)
    bias_c = jnp.concatenate([_bias_lookup(bucket, tab_ref, kh * G + g) for g in range(G)], axis=0)
    vis = jnp.concatenate([dist >= 0] * G, axis=0)
    sc = _dot_nt(qg, kc_ref[...].astype(BF16)) * ATTN_SCALE + bias_c
    _, lc, pc = _softmax_first(sc, vis)
    pc = pc / jnp.maximum(lc, 1e-30)
    oc = _dot(pc.astype(BF16), vc_ref[...].astype(BF16))

    pcs = pc[0:TQ]
    for g in range(1, G):
        pcs = pcs + pc[g * TQ:(g + 1) * TQ]
    score = _dot_nt(_stride_to_block_matrix(n_cmp, n_blk, True), pcs, precision=HIGHEST)
    jb = _iota((n_blk, TQ), 0)
    cur = (q0 + _iota((n_blk, TQ), 1)) // SEL_BLOCK
    forced = (jb == 0) | (jb == cur) | (jb == cur - 1)
    score = jnp.where(jb <= cur, score + jnp.where(forced, FORCE_BONUS, 0.0), -jnp.inf)
    rank = jnp.zeros((n_blk, TQ), I32)
    for m in range(n_blk):
        c = score[m:m + 1, :]
        rank = rank + ((c > score) | ((c == score) & (jb > m))).astype(I32)
    sel = ((rank < SEL_TOPN) & (jb <= cur)).astype(F32).T.astype(BF16)

    row = _iota((TQ, TQ), 0)
    col = _iota((TQ, TQ), 1)

    def tile(k_ref, v_ref, j, kind):
        sl = pl.ds(pl.multiple_of(j * TQ, TQ), TQ)
        s = _dot_nt(qg, k_ref[sl, :].astype(BF16)) * ATTN_SCALE + bias_ref[kind].reshape(G * TQ, TQ)
        return s, v_ref[sl, :].astype(BF16)

    def first(k_ref, v_ref, mask):
        s, vb = tile(k_ref, v_ref, i, 0)
        s = jnp.where(jnp.concatenate([mask] * G, axis=0), s, NEG)
        m = jnp.max(s, axis=1, keepdims=True)
        p = jnp.exp(s - m)
        return m, jnp.sum(p, axis=1, keepdims=True), _dot(p.astype(BF16), vb)

    def step(k_ref, v_ref, j, mask, carry):
        m, l, acc = carry
        s, vb = tile(k_ref, v_ref, j, jnp.minimum(i - j, 2))
        s = jnp.where(jnp.concatenate([mask] * G, axis=0), s, NEG)
        m_new = jnp.maximum(m, jnp.max(s, axis=1, keepdims=True))
        a = jnp.exp(m - m_new)
        p = jnp.exp(s - m_new)
        return m_new, a * l + jnp.sum(p, axis=1, keepdims=True), a * acc + _dot(p.astype(BF16), vb)

    def picked(j):
        e = (_iota((n_blk, TQ), 0) == j * (TQ // SEL_BLOCK) + _iota((n_blk, TQ), 1) // SEL_BLOCK)
        return _dot(sel, e.astype(BF16)) > 0.5

    carry = first(ks_ref, vs_ref, picked(i) & (col <= row))
    ms, ls, accs = lax.fori_loop(0, i, lambda j, c: step(ks_ref, vs_ref, j, picked(j), c), carry)
    osel = accs / ls

    carry = first(kw_ref, vw_ref, col <= row)

    def wbody(d, c):
        mask = (col - row) > jnp.where(d == 1, -TQ, 0)
        return step(kw_ref, vw_ref, i - d, mask, c)

    mw, lw, accw = lax.fori_loop(1, jnp.minimum(i, WINDOW // TQ) + 1, wbody, carry)
    ow = accw / lw

    gs = jax.nn.sigmoid(g_ref[...])
    for g in range(G):
        rs = slice(g * TQ, (g + 1) * TQ)
        o_ref[:, g * HEAD_DIM:(g + 1) * HEAD_DIM] = (
            gs[:, 3 * g:3 * g + 1] * oc[rs] + gs[:, 3 * g + 1:3 * g + 2] * osel[rs] + gs[:, 3 * g + 2:3 * g + 3] * ow[rs])


def _nsa_prompt_attn(z, gates, cmp, bias, rel_bias, nbatch, t):
    nq = t // TQ
    n_cmp = t // CMP_STRIDE
    kv0 = HD // HEAD_DIM
    kv_spec = lambda kind: pl.BlockSpec((t, HEAD_DIM), lambda b, k, i: (b, kv0 + kind * NSA_KV_HEADS + k))
    return pl.pallas_call(
        _nsa_prompt_kernel,
        out_shape=jax.ShapeDtypeStruct((nbatch * t, HD), F32),
        grid=(nbatch, NSA_KV_HEADS, nq),
        in_specs=[
            pl.BlockSpec(memory_space=pltpu.SMEM),
            pl.BlockSpec((TQ, NSA_GROUP * HEAD_DIM), lambda b, k, i: (b * nq + i, k)),
            pl.BlockSpec((None, TQ, 3 * NSA_GROUP), lambda b, k, i: (k, b * nq + i, 0)),
            pl.BlockSpec((None, None, n_cmp, HEAD_DIM), lambda b, k, i: (b, 0, 0, k)),
            pl.BlockSpec((None, None, n_cmp, HEAD_DIM), lambda b, k, i: (b, 1, 0, k)),
            kv_spec(2), kv_spec(3), kv_spec(4), kv_spec(5),
            pl.BlockSpec((3, NSA_GROUP, TQ, TQ), lambda b, k, i: (0, k, 0, 0)),
        ],
        out_specs=pl.BlockSpec((TQ, NSA_GROUP * HEAD_DIM), lambda b, k, i: (b * nq + i, k)),
        compiler_params=_params(("parallel", "parallel", "arbitrary")),
        name="nsa_prompt_attn",
    )(rel_bias, z, gates, cmp, cmp, z, z, z, z, bias)


def _nsa_sample_kernel(pt_ref, q_ref, g_ref, cmp_ref, bc_ref, bs_ref, bw_ref, wall_ref, new_ref, kpg_ref, vpg_ref,
                       o_ref, qb_ref, sel_ref, m_ref, l_ref, acc_ref, ocw_ref, *, n_pages, s_len, past_len):
    p = pl.program_id(1)
    rows = N_HEADS * s_len
    n_cmp = cmp_ref.shape[1]
    n_sel = sel_ref.shape[1]
    jb = _iota((rows, n_sel), 1)

    def update(first, k, v, kind, mask):
        s = _dot_nt(qb_ref[...], k.astype(BF16)) * ATTN_SCALE + bs_ref[kind]
        if first:
            m, l, pr = _softmax_first(s, mask)
            acc_ref[...] = _dot(pr.astype(BF16), v.astype(BF16))
        else:
            m, l, pr, a = _softmax_next(s, mask, m_ref[...], l_ref[...])
            acc_ref[...] = a * acc_ref[...] + _dot(pr.astype(BF16), v.astype(BF16))
        m_ref[...] = m
        l_ref[...] = l

    def picked(blk0):
        e = (_iota((n_sel, PAGE), 0) == blk0 + _iota((n_sel, PAGE), 1) // SEL_BLOCK)
        return _dot(sel_ref[...].astype(BF16), e.astype(BF16)) > 0.5

    @pl.when(p == 0)
    def _():
        q = q_ref[...]
        blocks = []
        for h in range(N_HEADS):
            lane_ok = (_iota((s_len, KD), 1) // HEAD_DIM) == (h // NSA_GROUP)
            qh = q[:, h * HEAD_DIM:(h + 1) * HEAD_DIM]
            blocks.append(jnp.where(lane_ok, jnp.concatenate([qh] * NSA_KV_HEADS, axis=1), 0.0))
        qb = jnp.concatenate(blocks, axis=0).astype(BF16)
        qb_ref[...] = qb
        gs = jax.nn.sigmoid(g_ref[...])

        t_c = _iota((rows, n_cmp), 0) % s_len
        vis = CMP_STRIDE * _iota((rows, n_cmp), 1) + (CMP_LEN - 1) <= past_len + t_c
        sc = _dot_nt(qb, cmp_ref[0].astype(BF16)) * ATTN_SCALE + bc_ref[...]
        _, lc, pc = _softmax_first(sc, vis)
        pc = pc / jnp.maximum(lc, 1e-30)
        oc = _dot(pc.astype(BF16), cmp_ref[1].astype(BF16))

        r = _iota((rows, rows), 0)
        c = _iota((rows, rows), 1)
        same = ((r // (NSA_GROUP * s_len)) == (c // (NSA_GROUP * s_len))) & ((r % s_len) == (c % s_len))
        pcs = _dot(same.astype(F32), pc, precision=HIGHEST)
        score = _dot(pcs, _stride_to_block_matrix(n_cmp, n_sel, False), precision=HIGHEST)
        cur = (past_len + _iota((rows, n_sel), 0) % s_len) // SEL_BLOCK
        forced = (jb == 0) | (jb == cur) | (jb == cur - 1)
        score = jnp.where(jb <= cur, score + jnp.where(forced, FORCE_BONUS, 0.0), -jnp.inf)
        sel_ref[...] = (_topk_mask(score, jb, SEL_TOPN) & (jb <= cur)).astype(F32)

        n_w = wall_ref.shape[0]
        dw = WINDOW + _iota((rows, n_w), 0) % s_len - _iota((rows, n_w), 1)
        sw = _dot_nt(qb, wall_ref[:, 0:KD].astype(BF16)) * ATTN_SCALE + bw_ref[...]
        _, lw, pw = _softmax_first(sw, (dw >= 0) & (dw < WINDOW))
        pw = pw / jnp.maximum(lw, 1e-30)
        ow = _dot(pw.astype(BF16), wall_ref[:, KD:2 * KD].astype(BF16))
        ocw_ref[...] = gs[:, 0:1] * oc + gs[:, 2:3] * ow

        t_s = _iota((rows, PAGE), 0) % s_len
        mask = picked(past_len // SEL_BLOCK) & (_iota((rows, PAGE), 1) <= t_s)
        update(True, new_ref[:, 0:KD], new_ref[:, KD:2 * KD], 0, mask)

    @pl.when(p > 0)
    def _():
        pg = p - 1
        update(False, kpg_ref[...], vpg_ref[...], jnp.minimum(n_pages - pg, 2), picked(pg * (PAGE // SEL_BLOCK)))

    @pl.when(p == n_pages)
    def _():
        gs = jax.nn.sigmoid(g_ref[...])
        o = ocw_ref[...] + gs[:, 1:2] * (acc_ref[...] / jnp.maximum(l_ref[...], 1e-30))
        for h in range(N_HEADS):
            k = h // NSA_GROUP
            o_ref[:, h * HEAD_DIM:(h + 1) * HEAD_DIM] = o[h * s_len:(h + 1) * s_len, k * HEAD_DIM:(k + 1) * HEAD_DIM]


def _nsa_sample_attn(z, gates, cmp, bias_c, bias_s, bias_w, wall, new_rows, cache, layer, page_table, s_len, past_len):
    nb_, n_pages = page_table.shape
    rows = N_HEADS * s_len
    n_cmp = cmp.shape[2]
    n_sel = 384
    assert past_len // SEL_BLOCK + 1 <= n_sel
    n_w = wall.shape[1]
    cache5 = cache.reshape(cache.shape[0], cache.shape[1], PAGE, 4 * KD)
    pg_spec = lambda e: pl.BlockSpec(
        (None, None, PAGE, KD), lambda b, p, pt: (layer, pt[b, jnp.maximum(p - 1, 0)], 0, e))
    full = lambda *shape: pl.BlockSpec(shape, lambda b, p, pt: (0,) * len(shape))
    return pl.pallas_call(
        functools.partial(_nsa_sample_kernel, n_pages=n_pages, s_len=s_len, past_len=past_len),
        out_shape=jax.ShapeDtypeStruct((nb_ * s_len, HD), F32),
        grid_spec=pltpu.PrefetchScalarGridSpec(
            num_scalar_prefetch=1, grid=(nb_, n_pages + 1),
            in_specs=[
                pl.BlockSpec((s_len, HD), lambda b, p, pt: (b, 0)),
                pl.BlockSpec((None, rows, 3), lambda b, p, pt: (b, 0, 0)),
                pl.BlockSpec((None, 2, n_cmp, KD), lambda b, p, pt: (b, 0, 0, 0)),
                full(rows, n_cmp), full(3, rows, PAGE), full(rows, n_w),
                pl.BlockSpec((None, n_w, 2 * KD), lambda b, p, pt: (b, 0, 0)),
                pl.BlockSpec((None, PAGE, 2 * KD), lambda b, p, pt: (b, 0, 0)),
                pg_spec(2), pg_spec(3),
            ],
            out_specs=pl.BlockSpec((s_len, HD), lambda b, p, pt: (b, 0)),
            scratch_shapes=[
                pltpu.VMEM((rows, KD), BF16), pltpu.VMEM((rows, n_sel), F32),
                pltpu.VMEM((rows, 1), F32), pltpu.VMEM((rows, 1), F32),
                pltpu.VMEM((rows, KD), F32), pltpu.VMEM((rows, KD), F32)]),
        compiler_params=_params(("parallel", "arbitrary")),
        name="nsa_sample_attn",
    )(page_table, z, gates, cmp, bias_c, bias_s, bias_w, wall, new_rows, cache5, cache5)


def _pad_cols(w, n):
    return jnp.pad(w, ((0, 0), (0, n - w.shape[1])))


def _tile_gain(parts):
    flags, gains = [], []
    for g, nh in parts:
        flags.append(jnp.full((nh * HEAD_DIM,), 0.0 if g is None else 1.0, F32))
        gains.append(jnp.ones((nh * HEAD_DIM,), F32) if g is None else jnp.tile(g.astype(F32), nh))
    return jnp.concatenate(flags)[None, :], jnp.concatenate(gains)[None, :]


def _half_ffn(xp, xs, g, wg, wu, wd):
    wg = _pad_cols(wg.astype(BF16), D_FF_PAD)
    wu = _pad_cols(wu.astype(BF16), D_FF_PAD)
    wd = jnp.pad(wd.astype(BF16), ((0, D_FF_PAD - D_FF), (0, 0)))
    return _ffn(xp, g, wg, wu, wd, tm=512), _ffn(xs, g, wg, wu, wd, tm=xs.shape[0])


def _moba_layer(xp, xs, nbatch, t, sbatch, s_len, g_mix, w_qkv, w_o, g_q, g_k, cache, layer, page_table, bias_p, bias_s):
    flag, gain = _tile_gain([(g_q, N_HEADS), (g_k, N_HEADS), (None, N_HEADS)])
    w = w_qkv.astype(BF16)
    wo = w_o.astype(BF16)
    yp = _proj(xp, g_mix, w, flag, gain, tm=512, tn=512)
    ys = _proj(xs, g_mix, w, flag, gain, tm=xs.shape[0], tn=512)

    ident = jnp.arange(nbatch * t // PAGE, dtype=I32).reshape(nbatch, t // PAGE)
    km_p = _kmean(yp.reshape(1, nbatch * t // PAGE, PAGE, 3 * HD), 0, 1, ident)
    op = _moba_prompt_attn(yp, km_p, bias_p, nbatch, t)
    xp = _oproj(op, wo, xp, tm=512)

    km_s = _kmean(cache.reshape(cache.shape[0], cache.shape[1], PAGE, 2 * HD), layer, 0, page_table)
    kv_new = jnp.pad(ys[:, HD:].reshape(sbatch, s_len, 2 * HD), ((0, 0), (0, PAGE - s_len), (0, 0)))
    osm = _moba_sample_attn(ys, km_s, bias_s, kv_new, cache, layer, page_table, s_len)
    xs = _oproj(osm, wo, xs, tm=xs.shape[0])

    kv_p = yp[:, HD:].reshape(nbatch, t, 2, N_HEADS, HEAD_DIM)
    kv_s = ys[:, HD:].reshape(sbatch, s_len, 2, N_HEADS, HEAD_DIM)
    return xp, xs, kv_p, kv_s


def _nsa_layer(xp, xs, nbatch, t, sbatch, s_len, past_len, g_mix, w_in, w_o, g_q, g_k, cmp_pos, cmp_w1, cmp_b1, cmp_w2,
               cache, win_buf, layer, page_table, rel_bias, bias_p, bias_s, bias_c, bias_w):
    flag, gain = _tile_gain([(g_q, N_HEADS), (None, 2 * NSA_KV_HEADS), (g_k[1], NSA_KV_HEADS), (None, NSA_KV_HEADS),
                             (g_k[2], NSA_KV_HEADS), (None, NSA_KV_HEADS)])
    extra = NSA_IN_PAD - flag.shape[1]
    flag = jnp.pad(flag, ((0, 0), (0, extra)))
    gain = jnp.pad(gain, ((0, 0), (0, extra)), constant_values=1.0)
    w = _pad_cols(w_in.astype(BF16), NSA_IN_PAD)
    wo = w_o.astype(BF16)
    zp = _proj(xp, g_mix, w, flag, gain, tm=512, tn=256)
    zs = _proj(xs, g_mix, w, flag, gain, tm=xs.shape[0], tn=256)
    g0 = HD + 6 * KD

    half = CMP_STRIDE * HEAD_DIM
    wcat = jnp.concatenate([cmp_w1[:, :half], cmp_w1[:, half:]], axis=2).astype(BF16)
    pos_flat = jnp.pad(cmp_pos.reshape(2, 1, CMP_LEN * HEAD_DIM), ((0, 0), (0, 7), (0, 0)))
    c0 = _cmp_c0(pos_flat, cmp_w1, cmp_b1.reshape(2, 1, CMP_HIDDEN))
    w2 = cmp_w2.astype(BF16)
    cflag = jnp.stack([jnp.ones((1, HEAD_DIM), F32), jnp.zeros((1, HEAD_DIM), F32)])
    cgain = jnp.stack([g_k[0].astype(F32)[None, :], jnp.ones((1, HEAD_DIM), F32)])

    ident = jnp.arange(nbatch * t // PAGE, dtype=I32).reshape(nbatch, t // PAGE)
    a_p = _cmp_a(zp.reshape(1, nbatch * t // PAGE, PAGE, NSA_IN_PAD), 0, HD // (2 * KD), ident, wcat)
    cmp_p = _cmp_b(a_p, c0, w2, cflag, cgain)
    gates_p = zp[:, g0:g0 + 3 * N_HEADS].reshape(nbatch * t, NSA_KV_HEADS, 3 * NSA_GROUP).transpose(1, 0, 2)
    op = _nsa_prompt_attn(zp, gates_p, cmp_p, bias_p, rel_bias, nbatch, t)
    xp = _oproj(op, wo, xp, tm=512)

    a_s = _cmp_a(cache.reshape(cache.shape[0], cache.shape[1], PAGE, 4 * KD), layer, 0, page_table, wcat)
    cmp_s = _cmp_b(a_s, c0, w2, cflag, cgain)
    gates_s = zs[:, g0:g0 + 3 * N_HEADS].reshape(sbatch, s_len, N_HEADS, 3).transpose(0, 2, 1, 3)
    gates_s = gates_s.reshape(sbatch, N_HEADS * s_len, 3)
    win_new = zs[:, HD + 4 * KD:g0].reshape(sbatch, s_len, 2 * KD)
    w_all = jnp.concatenate([win_buf.reshape(sbatch, -1, 2 * KD), win_new], axis=1)
    n_w = w_all.shape[1]
    wall = jnp.pad(w_all, ((0, 0), (0, bias_w.shape[1] - n_w), (0, 0)))
    new_rows = jnp.pad(zs[:, HD + 2 * KD:HD + 4 * KD].reshape(sbatch, s_len, 2 * KD), ((0, 0), (0, PAGE - s_len), (0, 0)))
    osm = _nsa_sample_attn(zs, gates_s, cmp_s, bias_c, bias_s, bias_w, wall, new_rows, cache, layer, page_table,
                           s_len, past_len)
    xs = _oproj(osm, wo, xs, tm=xs.shape[0])

    rows_p = zp[:, HD:HD + 4 * KD].reshape(nbatch, t, 4, NSA_KV_HEADS, HEAD_DIM)
    rows_s = zs[:, HD:HD + 4 * KD].reshape(sbatch, s_len, 4, NSA_KV_HEADS, HEAD_DIM)
    keep = min(WINDOW, t)
    win_p = zp[:, HD + 4 * KD:g0].reshape(nbatch, t, 2, NSA_KV_HEADS, HEAD_DIM)[:, t - keep:]
    win_s = w_all[:, s_len:].reshape(sbatch, n_w - s_len, 2, NSA_KV_HEADS, HEAD_DIM)
    return xp, xs, rows_p, rows_s, win_p, win_s


def kernel(x_prompt, x_sample, cache_moba_kv, cache_nsa_kv, state_nsa_win, page_table, rel_bias, ffn_norm, ffn_w_gate, ffn_w_up, ffn_w_down, mix_norm, moba_w_qkv, moba_w_o, moba_q_norm, moba_k_norm, nsa_w_in, nsa_w_o, nsa_q_norm, nsa_k_norm, nsa_cmp_pos, nsa_cmp_w1, nsa_cmp_b1, nsa_cmp_w2):
    nbatch, t, _ = x_prompt.shape
    sbatch, s_len, _ = x_sample.shape
    past_len = page_table.shape[1] * PAGE
    wb = state_nsa_win.shape[2]
    assert t % TQ == 0 and past_len % TQ == 0 and wb == WINDOW and s_len <= 8
    xp = x_prompt.reshape(nbatch * t, D_MODEL)
    xs = x_sample.reshape(sbatch * s_len, D_MODEL)
    rows = N_HEADS * s_len

    bias_p = _bias_tiles(rel_bias, 3, TQ, TQ, 0, TQ)
    bias_s = _bias_tiles(rel_bias, 3, s_len, PAGE, 0, PAGE).reshape(3, rows, PAGE)
    n_cmp_s = past_len // CMP_STRIDE
    bias_c = _bias_tiles(rel_bias, 1, s_len, n_cmp_s, past_len - (CMP_LEN - 1), 0, CMP_STRIDE).reshape(rows, n_cmp_s)
    n_w = -(-(wb + s_len) // PAGE) * PAGE
    bias_w = _bias_tiles(rel_bias, 1, s_len, n_w, wb, 0).reshape(rows, n_w)

    moba_p, moba_s, nsa_p, nsa_s, win_p, win_s = [], [], [], [], [], []
    for i in range(DEPTH):
        j = i // 2
        xp, xs = _half_ffn(xp, xs, ffn_norm[i, 0], ffn_w_gate[i, 0], ffn_w_up[i, 0], ffn_w_down[i, 0])
        if i % 2 == 0:
            xp, xs, kv_p, kv_s = _moba_layer(xp, xs, nbatch, t, sbatch, s_len, mix_norm[i], moba_w_qkv[j], moba_w_o[j],
                                             moba_q_norm[j], moba_k_norm[j], cache_moba_kv, j, page_table, bias_p, bias_s)
            moba_p.append(kv_p)
            moba_s.append(kv_s)
        else:
            xp, xs, rp, rs, wp, ws = _nsa_layer(xp, xs, nbatch, t, sbatch, s_len, past_len, mix_norm[i], nsa_w_in[j],
                                                nsa_w_o[j], nsa_q_norm[j], nsa_k_norm[j], nsa_cmp_pos[j], nsa_cmp_w1[j],
                                                nsa_cmp_b1[j], nsa_cmp_w2[j], cache_nsa_kv, state_nsa_win[j], j,
                                                page_table, rel_bias, bias_p, bias_s, bias_c, bias_w)
            nsa_p.append(rp)
            nsa_s.append(rs)
            win_p.append(wp)
            win_s.append(ws)
        xp, xs = _half_ffn(xp, xs, ffn_norm[i, 1], ffn_w_gate[i, 1], ffn_w_up[i, 1], ffn_w_down[i, 1])
    return (xp.reshape(nbatch, t, D_MODEL), xs.reshape(sbatch, s_len, D_MODEL), jnp.stack(moba_p), jnp.stack(moba_s),
            jnp.stack(nsa_p), jnp.stack(nsa_s), jnp.stack(win_p), jnp.stack(win_s))
```

```python
import functools
import math

import jax
import jax.numpy as jnp
from jax import lax
from jax.experimental import pallas as pl
from jax.experimental.pallas import tpu as pltpu

F32 = jnp.float32
BF16 = jnp.bfloat16
I32 = jnp.int32
HIGHEST = lax.Precision.HIGHEST

D_MODEL = 2048
DEPTH = 4
HEAD_DIM = 128
N_HEADS = 16
HD = N_HEADS * HEAD_DIM
D_FF = 5504
D_FF_PAD = 5632
MOBA_BLOCK = 256
MOBA_TOPK = 3
NSA_KV_HEADS = 4
NSA_GROUP = 4
KD = NSA_KV_HEADS * HEAD_DIM
CMP_LEN = 32
CMP_STRIDE = 16
CMP_HIDDEN = 256
SEL_BLOCK = 64
SEL_TOPN = 16
WINDOW = 512
FORCE_BONUS = 1e4
N_BUCKETS = 32
MAX_DISTANCE = 128
RMS_EPS = 1e-6
ATTN_SCALE = HEAD_DIM ** -0.5
PAGE = 128
TQ = 256
NSA_IN = HD + 6 * KD + 3 * N_HEADS
NSA_IN_PAD = 5376
NEG = -1e30
VMEM_LIMIT = 56 * 1024 * 1024

NT = (((1,), (1,)), ((), ()))


def _params(sem, vmem=VMEM_LIMIT):
    return pltpu.CompilerParams(dimension_semantics=sem, vmem_limit_bytes=vmem)


def _dot(a, b, **kw):
    return jnp.dot(a, b, preferred_element_type=F32, **kw)


def _dot_nt(a, b, **kw):
    return lax.dot_general(a, b, NT, preferred_element_type=F32, **kw)


def _iota(shape, dim):
    return lax.broadcasted_iota(I32, shape, dim)


def _rms_scale(x):
    return lax.rsqrt(jnp.mean(x * x, axis=-1, keepdims=True) + RMS_EPS)


def _t5_bucket(dist):
    n = jnp.maximum(dist, 0)
    exact = N_BUCKETS // 2
    nf = jnp.maximum(n, 1).astype(F32)
    log_b = exact + (jnp.log(nf * (1.0 / exact)) / math.log(MAX_DISTANCE / exact) * (N_BUCKETS - exact)).astype(I32)
    return jnp.where(n < exact, n, jnp.minimum(log_b, N_BUCKETS - 1))


def _bias_lookup(bucket, tab_ref, h):
    acc = jnp.zeros(bucket.shape, F32)
    for b in range(N_BUCKETS):
        acc = jnp.where(bucket == b, tab_ref[b, h], acc)
    return acc


def _topk_mask(score, idx, k):
    rank = jnp.zeros(score.shape, I32)
    for m in range(score.shape[1]):
        c = score[:, m:m + 1]
        beats = (c > score) | ((c == score) & (idx > m))
        rank = rank + beats.astype(I32)
    return rank < k


def _pick_col(mat, idx, j):
    return jnp.sum(jnp.where(idx == j, mat, 0.0), axis=1, keepdims=True)


def _softmax_first(s, mask):
    s = jnp.where(mask, s, NEG)
    m = jnp.max(s, axis=1, keepdims=True)
    p = jnp.where(mask, jnp.exp(s - m), 0.0)
    return m, jnp.sum(p, axis=1, keepdims=True), p


def _softmax_next(s, mask, m, l):
    s = jnp.where(mask, s, NEG)
    m_new = jnp.maximum(m, jnp.max(s, axis=1, keepdims=True))
    a = jnp.exp(m - m_new)
    p = jnp.where(mask, jnp.exp(s - m_new), 0.0)
    return m_new, a * l + jnp.sum(p, axis=1, keepdims=True), p, a


def _ffn_kernel(x_ref, g_ref, wg_ref, wu_ref, wd_ref, o_ref, h_ref, acc_ref):
    j = pl.program_id(1)

    @pl.when(j == 0)
    def _():
        x = x_ref[...]
        h_ref[...] = (x * _rms_scale(x) * g_ref[...]).astype(BF16)
        acc_ref[...] = jnp.zeros_like(acc_ref)

    h = h_ref[...]
    a = _dot(h, wg_ref[...])
    b = _dot(h, wu_ref[...])
    t = (a * jax.nn.sigmoid(a) * b).astype(BF16)
    acc_ref[...] += _dot(t, wd_ref[...])

    @pl.when(j == pl.num_programs(1) - 1)
    def _():
        o_ref[...] = x_ref[...] + 0.5 * acc_ref[...]


def _ffn(x, g, wg, wu, wd, tm, tf=512):
    n = x.shape[0]
    fp = wg.shape[1]
    return pl.pallas_call(
        _ffn_kernel,
        out_shape=jax.ShapeDtypeStruct((n, D_MODEL), F32),
        grid=(n // tm, fp // tf),
        in_specs=[
            pl.BlockSpec((tm, D_MODEL), lambda i, j: (i, 0)),
            pl.BlockSpec((1, D_MODEL), lambda i, j: (0, 0)),
            pl.BlockSpec((D_MODEL, tf), lambda i, j: (0, j)),
            pl.BlockSpec((D_MODEL, tf), lambda i, j: (0, j)),
            pl.BlockSpec((tf, D_MODEL), lambda i, j: (j, 0)),
        ],
        out_specs=pl.BlockSpec((tm, D_MODEL), lambda i, j: (i, 0)),
        scratch_shapes=[pltpu.VMEM((tm, D_MODEL), BF16), pltpu.VMEM((tm, D_MODEL), F32)],
        compiler_params=_params(("parallel", "arbitrary")),
        name="ffn",
    )(x, g.reshape(1, D_MODEL), wg, wu, wd)


def _proj_kernel(x_ref, g_ref, w_ref, f_ref, gn_ref, o_ref, h_ref):
    @pl.when(pl.program_id(1) == 0)
    def _():
        x = x_ref[...]
        h_ref[...] = (x * _rms_scale(x) * g_ref[...]).astype(BF16)

    y = _dot(h_ref[...], w_ref[...])
    f = f_ref[...]
    gn = gn_ref[...]
    for c in range(y.shape[1] // HEAD_DIM):
        sl = slice(c * HEAD_DIM, (c + 1) * HEAD_DIM)
        yc = y[:, sl]
        sc = f[:, sl] * _rms_scale(yc) + (1.0 - f[:, sl])
        o_ref[:, sl] = yc * sc * gn[:, sl]


def _proj(x, g, w, flag, gain, tm, tn):
    n = x.shape[0]
    npad = w.shape[1]
    return pl.pallas_call(
        _proj_kernel,
        out_shape=jax.ShapeDtypeStruct((n, npad), F32),
        grid=(n // tm, npad // tn),
        in_specs=[
            pl.BlockSpec((tm, D_MODEL), lambda i, j: (i, 0)),
            pl.BlockSpec((1, D_MODEL), lambda i, j: (0, 0)),
            pl.BlockSpec((D_MODEL, tn), lambda i, j: (0, j)),
            pl.BlockSpec((1, tn), lambda i, j: (0, j)),
            pl.BlockSpec((1, tn), lambda i, j: (0, j)),
        ],
        out_specs=pl.BlockSpec((tm, tn), lambda i, j: (i, j)),
        scratch_shapes=[pltpu.VMEM((tm, D_MODEL), BF16)],
        compiler_params=_params(("parallel", "arbitrary")),
        name="proj",
    )(x, g.reshape(1, D_MODEL), w, flag, gain)


def _oproj_kernel(a_ref, w_ref, x_ref, o_ref):
    o_ref[...] = x_ref[...] + _dot(a_ref[...].astype(BF16), w_ref[...])


def _oproj(a, w, x, tm, tn=512):
    n = a.shape[0]
    return pl.pallas_call(
        _oproj_kernel,
        out_shape=jax.ShapeDtypeStruct((n, D_MODEL), F32),
        grid=(n // tm, D_MODEL // tn),
        in_specs=[
            pl.BlockSpec((tm, HD), lambda i, j: (i, 0)),
            pl.BlockSpec((HD, tn), lambda i, j: (0, j)),
            pl.BlockSpec((tm, tn), lambda i, j: (i, j)),
        ],
        out_specs=pl.BlockSpec((tm, tn), lambda i, j: (i, j)),
        compiler_params=_params(("parallel", "arbitrary")),
        name="oproj",
    )(a, w, x)


def _bias_kernel(tab_ref, o_ref, *, base, stride, cmul):
    h = pl.program_id(0)
    nk, rows, cols = o_ref.shape
    d0 = _iota((rows, cols), 0) - cmul * _iota((rows, cols), 1) + base
    for kind in range(nk):
        d = d0 + kind * stride
        o_ref[kind] = jnp.where(d < 0, NEG, _bias_lookup(_t5_bucket(d), tab_ref, h))


def _bias_tiles(rel_bias, nk, rows, cols, base, stride, cmul=1):
    return pl.pallas_call(
        functools.partial(_bias_kernel, base=base, stride=stride, cmul=cmul),
        out_shape=jax.ShapeDtypeStruct((nk, N_HEADS, rows, cols), F32),
        grid=(N_HEADS,),
        in_specs=[pl.BlockSpec(memory_space=pltpu.SMEM)],
        out_specs=pl.BlockSpec((nk, None, rows, cols), lambda h: (0, h, 0, 0)),
        compiler_params=_params(("arbitrary",)),
        name="bias_tiles",
    )(rel_bias)


def _kmean_kernel(pt_ref, k0_ref, k1_ref, o_ref):
    s = jnp.sum(k0_ref[...], axis=0, keepdims=True) + jnp.sum(k1_ref[...], axis=0, keepdims=True)
    o_ref[...] = s * (1.0 / MOBA_BLOCK)


def _kmean(pages, layer, kcol, page_table):
    nb_, npg = page_table.shape
    nblk = npg // 2
    spec = lambda e: pl.BlockSpec((None, None, PAGE, HD), lambda b, n, pt: (layer, pt[b, 2 * n + e], 0, kcol))
    out = pl.pallas_call(
        _kmean_kernel,
        out_shape=jax.ShapeDtypeStruct((nb_, nblk, 1, HD), F32),
        grid_spec=pltpu.PrefetchScalarGridSpec(
            num_scalar_prefetch=1, grid=(nb_, nblk),
            in_specs=[spec(0), spec(1)],
            out_specs=pl.BlockSpec((None, None, 1, HD), lambda b, n, pt: (b, n, 0, 0))),
        compiler_params=_params(("parallel", "arbitrary")),
        name="kmean",
    )(page_table, pages, pages)
    return out.reshape(nb_, nblk, HD)


def _kmean_cache_kernel(pt_ref, k0_ref, k1_ref, o_ref):
    o_ref[...] = (jnp.sum(k0_ref[...], axis=0) + jnp.sum(k1_ref[...], axis=0)) * (1.0 / MOBA_BLOCK)


def _kmean_cache(cache, layer, page_table):
    nb_, npg = page_table.shape
    nblk = npg // 2
    rows = cache.reshape(cache.shape[0], cache.shape[1], PAGE, 2 * N_HEADS, HEAD_DIM)
    spec = lambda e: pl.BlockSpec((None, None, PAGE, N_HEADS, HEAD_DIM),
                                  lambda b, n, pt: (layer, pt[b, 2 * n + e], 0, 0, 0))
    out = pl.pallas_call(
        _kmean_cache_kernel,
        out_shape=jax.ShapeDtypeStruct((nb_, nblk, N_HEADS, HEAD_DIM), F32),
        grid_spec=pltpu.PrefetchScalarGridSpec(
            num_scalar_prefetch=1, grid=(nb_, nblk),
            in_specs=[spec(0), spec(1)],
            out_specs=pl.BlockSpec((None, None, N_HEADS, HEAD_DIM), lambda b, n, pt: (b, n, 0, 0))),
        compiler_params=_params(("parallel", "arbitrary")),
        name="kmean_cache",
    )(page_table, rows, rows)
    return out.reshape(nb_, nblk, HD)


MOBA_HPS = 2


def _moba_prompt_kernel(q_ref, k_ref, v_ref, km_ref, bias_ref, o_ref):
    i = pl.program_id(2)
    nb = km_ref.shape[0]
    jb = _iota((nb, TQ), 0)
    past = jb < i
    lane = _iota((TQ, HEAD_DIM), 1)
    heads = [slice(e * HEAD_DIM, (e + 1) * HEAD_DIM) for e in range(MOBA_HPS)]

    q_aug = []
    for sl in heads:
        q = q_ref[:, sl]
        gate = jnp.where(past, _dot_nt(km_ref[:, sl], q, precision=HIGHEST), -jnp.inf)
        rank = jnp.zeros((nb, TQ), I32)
        for m in range(nb):
            c = gate[m:m + 1, :]
            rank = rank + ((c > gate) | ((c == gate) & (jb > m))).astype(I32)
        pen = jnp.where((rank < MOBA_TOPK) & past, 0.0, NEG)
        pen = jnp.concatenate([pen, jnp.zeros((HEAD_DIM - nb, TQ), F32)], axis=0).T
        q_aug.append(jnp.concatenate([(q * ATTN_SCALE).astype(BF16), pen.astype(BF16)], axis=1))

    def tile(e, j, hot, kind):
        rows = pl.ds(pl.multiple_of(j * TQ, TQ), TQ)
        k_aug = jnp.concatenate([k_ref[rows, heads[e]].astype(BF16), hot], axis=1)
        return _dot_nt(q_aug[e], k_aug) + bias_ref[kind, e], v_ref[rows, heads[e]].astype(BF16)

    carry = []
    for e in range(MOBA_HPS):
        s, vb = tile(e, i, jnp.zeros((TQ, HEAD_DIM), BF16), 0)
        m = jnp.max(s, axis=1, keepdims=True)
        p = jnp.exp(s - m)
        carry += [m, jnp.sum(p, axis=1, keepdims=True), _dot(p.astype(BF16), vb)]

    def body(j, carry):
        hot = jnp.where(lane == j, 1.0, 0.0).astype(BF16)
        kind = jnp.minimum(i - j, 2)
        out = []
        for e in range(MOBA_HPS):
            m, l, acc = carry[3 * e:3 * e + 3]
            s, vb = tile(e, j, hot, kind)
            m_new = jnp.maximum(m, jnp.max(s, axis=1, keepdims=True))
            a = jnp.exp(m - m_new)
            p = jnp.exp(s - m_new)
            out += [m_new, a * l + jnp.sum(p, axis=1, keepdims=True), a * acc + _dot(p.astype(BF16), vb)]
        return tuple(out)

    carry = lax.fori_loop(0, i, body, tuple(carry))
    for e in range(MOBA_HPS):
        o_ref[:, heads[e]] = carry[3 * e + 2] / carry[3 * e + 1]


def _moba_prompt_attn(y, kmean, bias, nbatch, t):
    nq = t // TQ
    assert nq <= HEAD_DIM
    w = MOBA_HPS * HEAD_DIM
    ngrp = N_HEADS // MOBA_HPS
    return pl.pallas_call(
        _moba_prompt_kernel,
        out_shape=jax.ShapeDtypeStruct((nbatch * t, HD), F32),
        grid=(nbatch, ngrp, nq),
        in_specs=[
            pl.BlockSpec((TQ, w), lambda b, h, i: (b * nq + i, h)),
            pl.BlockSpec((t, w), lambda b, h, i: (b, ngrp + h)),
            pl.BlockSpec((t, w), lambda b, h, i: (b, 2 * ngrp + h)),
            pl.BlockSpec((None, nq, w), lambda b, h, i: (b, 0, h)),
            pl.BlockSpec((3, MOBA_HPS, TQ, TQ), lambda b, h, i: (0, h, 0, 0)),
        ],
        out_specs=pl.BlockSpec((TQ, w), lambda b, h, i: (b * nq + i, h)),
        compiler_params=_params(("parallel", "parallel", "arbitrary")),
        name="moba_prompt_attn",
    )(y, y, y, kmean, bias)


def _gather_heads(r0, r1, h0, nh):
    return jnp.concatenate(
        [jnp.concatenate([r0[:, h0 + h, :], r1[:, h0 + h, :]], axis=0) for h in range(nh)], axis=1)


def _moba_sample_kernel(pt_ref, q_ref, km_ref, bias_ref, knew_ref, vnew_ref, k0_ref, k1_ref, v0_ref, v1_ref, o_ref,
                        qb_ref, sel_ref, m_ref, l_ref, acc_ref, *, s_len):
    p = pl.program_id(1)
    nblk = km_ref.shape[0]
    rows = N_HEADS * s_len
    nidx = _iota((rows, nblk), 1)

    def update(first, k, v, bias, mask):
        s = _dot_nt(qb_ref[...], k.astype(BF16)) * ATTN_SCALE + bias
        if first:
            m, l, pr = _softmax_first(s, mask)
            acc_ref[...] = _dot(pr.astype(BF16), v.astype(BF16))
        else:
            m, l, pr, a = _softmax_next(s, mask, m_ref[...], l_ref[...])
            acc_ref[...] = a * acc_ref[...] + _dot(pr.astype(BF16), v.astype(BF16))
        m_ref[...] = m
        l_ref[...] = l

    @pl.when(p == 0)
    def _():
        q = q_ref[...]
        blocks = []
        for h in range(N_HEADS):
            sl = slice(h * HEAD_DIM, (h + 1) * HEAD_DIM)
            lane_ok = (_iota((s_len, HD), 1) // HEAD_DIM) == h
            blocks.append(jnp.where(lane_ok, jnp.concatenate([q[:, sl]] * N_HEADS, axis=1), 0.0))
        qbd = jnp.concatenate(blocks, axis=0)
        gate = _dot_nt(qbd, km_ref[...], precision=HIGHEST)
        sel_ref[...] = _topk_mask(gate, nidx, MOBA_TOPK).astype(F32)
        qb_ref[...] = qbd.astype(BF16)
        t = _iota((rows, PAGE), 0) % s_len
        update(True, knew_ref[...], vnew_ref[...], bias_ref[0, :, 0:PAGE], _iota((rows, PAGE), 1) <= t)

    @pl.when(p > 0)
    def _():
        n = p - 1
        picked = _pick_col(sel_ref[...], nidx, n) > 0.5
        mask = jnp.broadcast_to(picked, (rows, MOBA_BLOCK))
        update(False, _gather_heads(k0_ref, k1_ref, 0, N_HEADS), _gather_heads(v0_ref, v1_ref, 0, N_HEADS),
               bias_ref[jnp.minimum(nblk - n, 2)], mask)

    @pl.when(p == nblk)
    def _():
        inv = 1.0 / l_ref[...]
        for h in range(N_HEADS):
            sl = slice(h * HEAD_DIM, (h + 1) * HEAD_DIM)
            rs = slice(h * s_len, (h + 1) * s_len)
            o_ref[:, sl] = acc_ref[rs, sl] * inv[rs]


def _moba_sample_attn(y, kmean, bias, kv_new, cache, layer, page_table, s_len):
    nb_, n_pages = page_table.shape
    nblk = n_pages // 2
    rows = N_HEADS * s_len
    cache5 = cache.reshape(cache.shape[0], cache.shape[1], PAGE, 2 * N_HEADS, HEAD_DIM)
    pg_spec = lambda kv, e: pl.BlockSpec(
        (None, None, PAGE, N_HEADS, HEAD_DIM),
        lambda b, p, pt: (layer, pt[b, 2 * jnp.maximum(p - 1, 0) + e], 0, kv, 0))
    return pl.pallas_call(
        functools.partial(_moba_sample_kernel, s_len=s_len),
        out_shape=jax.ShapeDtypeStruct((nb_ * s_len, HD), F32),
        grid_spec=pltpu.PrefetchScalarGridSpec(
            num_scalar_prefetch=1, grid=(nb_, nblk + 1),
            in_specs=[
                pl.BlockSpec((s_len, HD), lambda b, p, pt: (b, 0)),
                pl.BlockSpec((None, nblk, HD), lambda b, p, pt: (b, 0, 0)),
                pl.BlockSpec((3, rows, MOBA_BLOCK), lambda b, p, pt: (0, 0, 0)),
                pl.BlockSpec((None, PAGE, HD), lambda b, p, pt: (b, 0, 0)),
                pl.BlockSpec((None, PAGE, HD), lambda b, p, pt: (b, 0, 1)),
                pg_spec(0, 0), pg_spec(0, 1), pg_spec(1, 0), pg_spec(1, 1),
            ],
            out_specs=pl.BlockSpec((s_len, HD), lambda b, p, pt: (b, 0)),
            scratch_shapes=[
                pltpu.VMEM((rows, HD), BF16), pltpu.VMEM((rows, nblk), F32),
                pltpu.VMEM((rows, 1), F32), pltpu.VMEM((rows, 1), F32), pltpu.VMEM((rows, HD), F32)]),
        compiler_params=_params(("parallel", "arbitrary")),
        name="moba_sample_attn",
    )(page_table, y, kmean, bias, kv_new, kv_new, cache5, cache5, cache5, cache5)


def _cmp_a_kernel(pt_ref, *refs, pg):
    x_refs, w_ref, o_ref, lhs_ref = refs[:pg], refs[pg], refs[pg + 1], refs[pg + 2]
    n_str = PAGE // CMP_STRIDE
    for kind in range(2):
        for c in range(pg):
            for k in range(NSA_KV_HEADS):
                r0 = (k * pg + c) * n_str
                for l in range(CMP_STRIDE):
                    lhs_ref[kind, r0:r0 + n_str, l * HEAD_DIM:(l + 1) * HEAD_DIM] = (
                        x_refs[c][pl.ds(l, n_str, stride=CMP_STRIDE), kind * NSA_KV_HEADS + k, :])
        res = _dot(lhs_ref[kind].astype(BF16), w_ref[kind])
        for k in range(NSA_KV_HEADS):
            o_ref[kind, k] = res[k * pg * n_str:(k + 1) * pg * n_str]


def _cmp_a(pages, layer, page_table, wcat, pg=8):
    nb_, npg = page_table.shape
    n_str = PAGE // CMP_STRIDE
    spec = lambda c: pl.BlockSpec((None, None, PAGE, 2 * NSA_KV_HEADS, HEAD_DIM),
                                  lambda b, s, pt: (layer, pt[b, s * pg + c], 0, 0, 0))
    return pl.pallas_call(
        functools.partial(_cmp_a_kernel, pg=pg),
        out_shape=jax.ShapeDtypeStruct((nb_, 2, NSA_KV_HEADS, npg * n_str, 2 * CMP_HIDDEN), F32),
        grid_spec=pltpu.PrefetchScalarGridSpec(
            num_scalar_prefetch=1, grid=(nb_, npg // pg),
            in_specs=[spec(c) for c in range(pg)] + [
                pl.BlockSpec((2, CMP_STRIDE * HEAD_DIM, 2 * CMP_HIDDEN), lambda b, s, pt: (0, 0, 0))],
            out_specs=pl.BlockSpec((None, 2, NSA_KV_HEADS, pg * n_str, 2 * CMP_HIDDEN),
                                   lambda b, s, pt: (b, 0, 0, s, 0)),
            scratch_shapes=[pltpu.VMEM((2, NSA_KV_HEADS * pg * n_str, CMP_STRIDE * HEAD_DIM), F32)]),
        compiler_params=_params(("parallel", "arbitrary")),
        name="cmp_a",
    )(page_table, *([pages] * pg), wcat)


def _c0_kernel(p_ref, w_ref, b_ref, o_ref):
    o_ref[...] = _dot(p_ref[...], w_ref[...]) + b_ref[...]


def _cmp_c0(pos_flat, w1, b1):
    return pl.pallas_call(
        _c0_kernel,
        out_shape=jax.ShapeDtypeStruct((2, 8, CMP_HIDDEN), F32),
        grid=(2,),
        in_specs=[
            pl.BlockSpec((None, 8, CMP_LEN * HEAD_DIM), lambda i: (i, 0, 0)),
            pl.BlockSpec((None, CMP_LEN * HEAD_DIM, CMP_HIDDEN), lambda i: (i, 0, 0)),
            pl.BlockSpec((None, 1, CMP_HIDDEN), lambda i: (i, 0, 0)),
        ],
        out_specs=pl.BlockSpec((None, 8, CMP_HIDDEN), lambda i: (i, 0, 0)),
        compiler_params=_params(("arbitrary",)),
        name="cmp_c0",
    )(pos_flat, w1, b1)


def _cmp_b_kernel(a_ref, c0_ref, w2_ref, f_ref, g_ref, o_ref):
    n = a_ref.shape[0]
    a0 = a_ref[:, 0:CMP_HIDDEN]
    a1 = pltpu.roll(a_ref[:, CMP_HIDDEN:2 * CMP_HIDDEN], n - 1, 0)
    h1 = a0 + a1 + c0_ref[0:1, :]
    y = _dot(jax.nn.gelu(h1).astype(BF16), w2_ref[...])
    f = f_ref[...]
    o_ref[...] = y * (f * _rms_scale(y) + (1.0 - f)) * g_ref[...]


def _cmp_b(a, c0, w2, flag, gain):
    nb_, _, _, n_str, _ = a.shape
    return pl.pallas_call(
        _cmp_b_kernel,
        out_shape=jax.ShapeDtypeStruct((nb_, 2, n_str, KD), F32),
        grid=(nb_, 2, NSA_KV_HEADS),
        in_specs=[
            pl.BlockSpec((None, None, None, n_str, 2 * CMP_HIDDEN), lambda b, c, k: (b, c, k, 0, 0)),
            pl.BlockSpec((None, 8, CMP_HIDDEN), lambda b, c, k: (c, 0, 0)),
            pl.BlockSpec((None, CMP_HIDDEN, HEAD_DIM), lambda b, c, k: (c, 0, 0)),
            pl.BlockSpec((None, 1, HEAD_DIM), lambda b, c, k: (c, 0, 0)),
            pl.BlockSpec((None, 1, HEAD_DIM), lambda b, c, k: (c, 0, 0)),
        ],
        out_specs=pl.BlockSpec((None, None, n_str, HEAD_DIM), lambda b, c, k: (b, c, 0, k)),
        compiler_params=_params(("parallel", "arbitrary", "arbitrary")),
        name="cmp_b",
    )(a, c0, w2, flag, gain)


def _stride_to_block_matrix(n_cmp, n_blk, transpose):
    shape = (n_blk, n_cmp) if transpose else (n_cmp, n_blk)
    n = _iota(shape, 1 if transpose else 0)
    j = _iota(shape, 0 if transpose else 1)
    r = n - (SEL_BLOCK // CMP_STRIDE) * j
    return jnp.where((r == -1) | (r == 3), 1.0, jnp.where((r >= 0) & (r <= 2), 2.0, 0.0)).astype(F32)


def _nsa_prompt_kernel(q_ref, g_ref, kc_ref, vc_ref, ks_ref, vs_ref, kw_ref, vw_ref, bias_ref, bc_ref, o_ref):
    i = pl.program_id(2)
    q0 = i * TQ
    G = NSA_GROUP
    n_cmp = kc_ref.shape[0]
    n_blk = ks_ref.shape[0] // SEL_BLOCK
    qg = jnp.concatenate([q_ref[:, g * HEAD_DIM:(g + 1) * HEAD_DIM] for g in range(G)], axis=0).astype(BF16)

    tq = q0 + _iota((TQ, n_cmp), 0)
    dist = tq - (CMP_STRIDE * _iota((TQ, n_cmp), 1) + CMP_LEN - 1)
    vis = jnp.concatenate([dist >= 0] * G, axis=0)
    sc = _dot_nt(qg, kc_ref[...].astype(BF16)) * ATTN_SCALE + bc_ref[...].reshape(G * TQ, n_cmp)
    _, lc, pc = _softmax_first(sc, vis)
    pc = pc / jnp.maximum(lc, 1e-30)
    oc = _dot(pc.astype(BF16), vc_ref[...].astype(BF16))

    pcs = pc[0:TQ]
    for g in range(1, G):
        pcs = pcs + pc[g * TQ:(g + 1) * TQ]
    score = _dot_nt(_stride_to_block_matrix(n_cmp, n_blk, True), pcs, precision=HIGHEST)
    jb = _iota((n_blk, TQ), 0)
    cur = (q0 + _iota((n_blk, TQ), 1)) // SEL_BLOCK
    forced = (jb == 0) | (jb == cur) | (jb == cur - 1)
    score = jnp.where(jb <= cur, score + jnp.where(forced, FORCE_BONUS, 0.0), -jnp.inf)
    rank = jnp.zeros((n_blk, TQ), I32)
    for m in range(n_blk):
        c = score[m:m + 1, :]
        rank = rank + ((c > score) | ((c == score) & (jb > m))).astype(I32)
    sel = ((rank < SEL_TOPN) & (jb <= cur)).astype(F32).T.astype(BF16)

    row = _iota((TQ, TQ), 0)
    col = _iota((TQ, TQ), 1)

    def tile(k_ref, v_ref, j, kind):
        sl = pl.ds(pl.multiple_of(j * TQ, TQ), TQ)
        s = _dot_nt(qg, k_ref[sl, :].astype(BF16)) * ATTN_SCALE + bias_ref[kind].reshape(G * TQ, TQ)
        return s, v_ref[sl, :].astype(BF16)

    def first(k_ref, v_ref, mask):
        s, vb = tile(k_ref, v_ref, i, 0)
        s = jnp.where(jnp.concatenate([mask] * G, axis=0), s, NEG)
        m = jnp.max(s, axis=1, keepdims=True)
        p = jnp.exp(s - m)
        return m, jnp.sum(p, axis=1, keepdims=True), _dot(p.astype(BF16), vb)

    def step(k_ref, v_ref, j, mask, carry):
        m, l, acc = carry
        s, vb = tile(k_ref, v_ref, j, jnp.minimum(i - j, 2))
        s = jnp.where(jnp.concatenate([mask] * G, axis=0), s, NEG)
        m_new = jnp.maximum(m, jnp.max(s, axis=1, keepdims=True))
        a = jnp.exp(m - m_new)
        p = jnp.exp(s - m_new)
        return m_new, a * l + jnp.sum(p, axis=1, keepdims=True), a * acc + _dot(p.astype(BF16), vb)

    def picked(j):
        e = (_iota((n_blk, TQ), 0) == j * (TQ // SEL_BLOCK) + _iota((n_blk, TQ), 1) // SEL_BLOCK)
        return _dot(sel, e.astype(BF16)) > 0.5

    carry = first(ks_ref, vs_ref, picked(i) & (col <= row))
    ms, ls, accs = lax.fori_loop(0, i, lambda j, c: step(ks_ref, vs_ref, j, picked(j), c), carry)
    osel = accs / ls

    carry = first(kw_ref, vw_ref, col <= row)

    def wbody(d, c):
        mask = (col - row) > jnp.where(d == 1, -TQ, 0)
        return step(kw_ref, vw_ref, i - d, mask, c)

    mw, lw, accw = lax.fori_loop(1, jnp.minimum(i, WINDOW // TQ) + 1, wbody, carry)
    ow = accw / lw

    gs = jax.nn.sigmoid(g_ref[...])
    for g in range(G):
        rs = slice(g * TQ, (g + 1) * TQ)
        o_ref[:, g * HEAD_DIM:(g + 1) * HEAD_DIM] = (
            gs[:, 3 * g:3 * g + 1] * oc[rs] + gs[:, 3 * g + 1:3 * g + 2] * osel[rs] + gs[:, 3 * g + 2:3 * g + 3] * ow[rs])


def _nsa_prompt_attn(z, gates, cmp, bias, bias_c, nbatch, t):
    nq = t // TQ
    n_cmp = t // CMP_STRIDE
    kv0 = HD // HEAD_DIM
    kv_spec = lambda kind: pl.BlockSpec((t, HEAD_DIM), lambda b, k, i: (b, kv0 + kind * NSA_KV_HEADS + k))
    return pl.pallas_call(
        _nsa_prompt_kernel,
        out_shape=jax.ShapeDtypeStruct((nbatch * t, HD), F32),
        grid=(nbatch, NSA_KV_HEADS, nq),
        in_specs=[
            pl.BlockSpec((TQ, NSA_GROUP * HEAD_DIM), lambda b, k, i: (b * nq + i, k)),
            pl.BlockSpec((None, TQ, 3 * NSA_GROUP), lambda b, k, i: (k, b * nq + i, 0)),
            pl.BlockSpec((None, None, n_cmp, HEAD_DIM), lambda b, k, i: (b, 0, 0, k)),
            pl.BlockSpec((None, None, n_cmp, HEAD_DIM), lambda b, k, i: (b, 1, 0, k)),
            kv_spec(2), kv_spec(3), kv_spec(4), kv_spec(5),
            pl.BlockSpec((3, NSA_GROUP, TQ, TQ), lambda b, k, i: (0, k, 0, 0)),
            pl.BlockSpec((None, NSA_GROUP, TQ, n_cmp), lambda b, k, i: (i, k, 0, 0)),
        ],
        out_specs=pl.BlockSpec((TQ, NSA_GROUP * HEAD_DIM), lambda b, k, i: (b * nq + i, k)),
        compiler_params=_params(("parallel", "parallel", "arbitrary")),
        name="nsa_prompt_attn",
    )(z, gates, cmp, cmp, z, z, z, z, bias, bias_c)


def _nsa_sample_kernel(pt_ref, q_ref, g_ref, cmp_ref, bc_ref, bs_ref, bw_ref, wall_ref, new_ref, pg0_ref, pg1_ref,
                       o_ref, qb_ref, sel_ref, m_ref, l_ref, acc_ref, ocw_ref, *, n_steps, s_len, past_len):
    p = pl.program_id(1)
    rows = N_HEADS * s_len
    n_cmp = cmp_ref.shape[1]
    n_sel = sel_ref.shape[1]
    jb = _iota((rows, n_sel), 1)

    def update(first, k, v, bias, mask):
        s = _dot_nt(qb_ref[...], k.astype(BF16)) * ATTN_SCALE + bias
        if first:
            m, l, pr = _softmax_first(s, mask)
            acc_ref[...] = _dot(pr.astype(BF16), v.astype(BF16))
        else:
            m, l, pr, a = _softmax_next(s, mask, m_ref[...], l_ref[...])
            acc_ref[...] = a * acc_ref[...] + _dot(pr.astype(BF16), v.astype(BF16))
        m_ref[...] = m
        l_ref[...] = l

    def picked(blk0, n_keys):
        e = (_iota((n_sel, n_keys), 0) == blk0 + _iota((n_sel, n_keys), 1) // SEL_BLOCK)
        return _dot(sel_ref[...].astype(BF16), e.astype(BF16)) > 0.5

    @pl.when(p == 0)
    def _():
        q = q_ref[...]
        blocks = []
        for h in range(N_HEADS):
            lane_ok = (_iota((s_len, KD), 1) // HEAD_DIM) == (h // NSA_GROUP)
            qh = q[:, h * HEAD_DIM:(h + 1) * HEAD_DIM]
            blocks.append(jnp.where(lane_ok, jnp.concatenate([qh] * NSA_KV_HEADS, axis=1), 0.0))
        qb = jnp.concatenate(blocks, axis=0).astype(BF16)
        qb_ref[...] = qb
        gs = jax.nn.sigmoid(g_ref[...])

        t_c = _iota((rows, n_cmp), 0) % s_len
        vis = CMP_STRIDE * _iota((rows, n_cmp), 1) + (CMP_LEN - 1) <= past_len + t_c
        sc = _dot_nt(qb, cmp_ref[0].astype(BF16)) * ATTN_SCALE + bc_ref[...]
        _, lc, pc = _softmax_first(sc, vis)
        pc = pc / jnp.maximum(lc, 1e-30)
        oc = _dot(pc.astype(BF16), cmp_ref[1].astype(BF16))

        r = _iota((rows, rows), 0)
        c = _iota((rows, rows), 1)
        same = ((r // (NSA_GROUP * s_len)) == (c // (NSA_GROUP * s_len))) & ((r % s_len) == (c % s_len))
        pcs = _dot(same.astype(F32), pc, precision=HIGHEST)
        score = _dot(pcs, _stride_to_block_matrix(n_cmp, n_sel, False), precision=HIGHEST)
        cur = (past_len + _iota((rows, n_sel), 0) % s_len) // SEL_BLOCK
        forced = (jb == 0) | (jb == cur) | (jb == cur - 1)
        score = jnp.where(jb <= cur, score + jnp.where(forced, FORCE_BONUS, 0.0), -jnp.inf)
        sel_ref[...] = (_topk_mask(score, jb, SEL_TOPN) & (jb <= cur)).astype(F32)

        n_w = wall_ref.shape[0]
        dw = WINDOW + _iota((rows, n_w), 0) % s_len - _iota((rows, n_w), 1)
        sw = _dot_nt(qb, wall_ref[:, 0:KD].astype(BF16)) * ATTN_SCALE + bw_ref[...]
        _, lw, pw = _softmax_first(sw, (dw >= 0) & (dw < WINDOW))
        pw = pw / jnp.maximum(lw, 1e-30)
        ow = _dot(pw.astype(BF16), wall_ref[:, KD:2 * KD].astype(BF16))
        ocw_ref[...] = gs[:, 0:1] * oc + gs[:, 2:3] * ow

        t_s = _iota((rows, PAGE), 0) % s_len
        mask = picked(past_len // SEL_BLOCK, PAGE) & (_iota((rows, PAGE), 1) <= t_s)
        update(True, new_ref[:, 0:KD], new_ref[:, KD:2 * KD], bs_ref[0, :, 0:PAGE], mask)

    @pl.when(p > 0)
    def _():
        n = p - 1
        update(False, _gather_heads(pg0_ref, pg1_ref, 0, NSA_KV_HEADS),
               _gather_heads(pg0_ref, pg1_ref, NSA_KV_HEADS, NSA_KV_HEADS),
               bs_ref[jnp.minimum(n_steps - n, 2)], picked(n * (2 * PAGE // SEL_BLOCK), 2 * PAGE))

    @pl.when(p == n_steps)
    def _():
        gs = jax.nn.sigmoid(g_ref[...])
        o = ocw_ref[...] + gs[:, 1:2] * (acc_ref[...] / jnp.maximum(l_ref[...], 1e-30))
        for h in range(N_HEADS):
            k = h // NSA_GROUP
            o_ref[:, h * HEAD_DIM:(h + 1) * HEAD_DIM] = o[h * s_len:(h + 1) * s_len, k * HEAD_DIM:(k + 1) * HEAD_DIM]


def _nsa_sample_attn(z, gates, cmp, bias_c, bias_s, bias_w, wall, new_rows, cache, layer, page_table, s_len, past_len):
    nb_, n_pages = page_table.shape
    rows = N_HEADS * s_len
    n_cmp = cmp.shape[2]
    n_sel = 384
    assert past_len // SEL_BLOCK + 1 <= n_sel
    n_w = wall.shape[1]
    n_steps = n_pages // 2
    cache5 = cache.reshape(cache.shape[0], cache.shape[1], PAGE, 4 * NSA_KV_HEADS, HEAD_DIM)
    pg_spec = lambda e: pl.BlockSpec(
        (None, None, PAGE, 2 * NSA_KV_HEADS, HEAD_DIM),
        lambda b, p, pt: (layer, pt[b, 2 * jnp.maximum(p - 1, 0) + e], 0, 1, 0))
    full = lambda *shape: pl.BlockSpec(shape, lambda b, p, pt: (0,) * len(shape))
    return pl.pallas_call(
        functools.partial(_nsa_sample_kernel, n_steps=n_steps, s_len=s_len, past_len=past_len),
        out_shape=jax.ShapeDtypeStruct((nb_ * s_len, HD), F32),
        grid_spec=pltpu.PrefetchScalarGridSpec(
            num_scalar_prefetch=1, grid=(nb_, n_steps + 1),
            in_specs=[
                pl.BlockSpec((s_len, HD), lambda b, p, pt: (b, 0)),
                pl.BlockSpec((None, rows, 3), lambda b, p, pt: (b, 0, 0)),
                pl.BlockSpec((None, 2, n_cmp, KD), lambda b, p, pt: (b, 0, 0, 0)),
                full(rows, n_cmp), full(3, rows, 2 * PAGE), full(rows, n_w),
                pl.BlockSpec((None, n_w, 2 * KD), lambda b, p, pt: (b, 0, 0)),
                pl.BlockSpec((None, PAGE, 2 * KD), lambda b, p, pt: (b, 0, 0)),
                pg_spec(0), pg_spec(1),
            ],
            out_specs=pl.BlockSpec((s_len, HD), lambda b, p, pt: (b, 0)),
            scratch_shapes=[
                pltpu.VMEM((rows, KD), BF16), pltpu.VMEM((rows, n_sel), F32),
                pltpu.VMEM((rows, 1), F32), pltpu.VMEM((rows, 1), F32),
                pltpu.VMEM((rows, KD), F32), pltpu.VMEM((rows, KD), F32)]),
        compiler_params=_params(("parallel", "arbitrary")),
        name="nsa_sample_attn",
    )(page_table, z, gates, cmp, bias_c, bias_s, bias_w, wall, new_rows, cache5, cache5)


def _pad_cols(w, n):
    return jnp.pad(w, ((0, 0), (0, n - w.shape[1])))


def _tile_gain(parts):
    flags, gains = [], []
    for g, nh in parts:
        flags.append(jnp.full((nh * HEAD_DIM,), 0.0 if g is None else 1.0, F32))
        gains.append(jnp.ones((nh * HEAD_DIM,), F32) if g is None else jnp.tile(g.astype(F32), nh))
    return jnp.concatenate(flags)[None, :], jnp.concatenate(gains)[None, :]


def _half_ffn(xp, xs, g, wg, wu, wd):
    wg = _pad_cols(wg.astype(BF16), D_FF_PAD)
    wu = _pad_cols(wu.astype(BF16), D_FF_PAD)
    wd = jnp.pad(wd.astype(BF16), ((0, D_FF_PAD - D_FF), (0, 0)))
    return _ffn(xp, g, wg, wu, wd, tm=512), _ffn(xs, g, wg, wu, wd, tm=xs.shape[0])


def _moba_layer(xp, xs, nbatch, t, sbatch, s_len, g_mix, w_qkv, w_o, g_q, g_k, cache, layer, page_table, bias_p, bias_s):
    flag, gain = _tile_gain([(g_q, N_HEADS), (g_k, N_HEADS), (None, N_HEADS)])
    w = w_qkv.astype(BF16)
    wo = w_o.astype(BF16)
    yp = _proj(xp, g_mix, w, flag, gain, tm=512, tn=512)
    ys = _proj(xs, g_mix, w, flag, gain, tm=xs.shape[0], tn=512)

    ident = jnp.arange(nbatch * t // PAGE, dtype=I32).reshape(nbatch, t // PAGE)
    km_p = _kmean(yp.reshape(1, nbatch * t // PAGE, PAGE, 3 * HD), 0, 1, ident)
    op = _moba_prompt_attn(yp, km_p, bias_p, nbatch, t)
    xp = _oproj(op, wo, xp, tm=512)

    km_s = _kmean_cache(cache, layer, page_table)
    kv_new = jnp.pad(ys[:, HD:].reshape(sbatch, s_len, 2 * HD), ((0, 0), (0, PAGE - s_len), (0, 0)))
    osm = _moba_sample_attn(ys, km_s, bias_s, kv_new, cache, layer, page_table, s_len)
    xs = _oproj(osm, wo, xs, tm=xs.shape[0])

    kv_p = yp[:, HD:].reshape(nbatch, t, 2, N_HEADS, HEAD_DIM)
    kv_s = ys[:, HD:].reshape(sbatch, s_len, 2, N_HEADS, HEAD_DIM)
    return xp, xs, kv_p, kv_s


def _nsa_layer(xp, xs, nbatch, t, sbatch, s_len, past_len, g_mix, w_in, w_o, g_q, g_k, cmp_pos, cmp_w1, cmp_b1, cmp_w2,
               cache, win_buf, layer, page_table, bias_p, bias_cp, bias_s, bias_c, bias_w):
    flag, gain = _tile_gain([(g_q, N_HEADS), (None, 2 * NSA_KV_HEADS), (g_k[1], NSA_KV_HEADS), (None, NSA_KV_HEADS),
                             (g_k[2], NSA_KV_HEADS), (None, NSA_KV_HEADS)])
    extra = NSA_IN_PAD - flag.shape[1]
    flag = jnp.pad(flag, ((0, 0), (0, extra)))
    gain = jnp.pad(gain, ((0, 0), (0, extra)), constant_values=1.0)
    w = _pad_cols(w_in.astype(BF16), NSA_IN_PAD)
    wo = w_o.astype(BF16)
    zp = _proj(xp, g_mix, w, flag, gain, tm=512, tn=256)
    zs = _proj(xs, g_mix, w, flag, gain, tm=xs.shape[0], tn=256)
    g0 = HD + 6 * KD

    half = CMP_STRIDE * HEAD_DIM
    wcat = jnp.concatenate([cmp_w1[:, :half], cmp_w1[:, half:]], axis=2).astype(BF16)
    pos_flat = jnp.pad(cmp_pos.reshape(2, 1, CMP_LEN * HEAD_DIM), ((0, 0), (0, 7), (0, 0)))
    c0 = _cmp_c0(pos_flat, cmp_w1, cmp_b1.reshape(2, 1, CMP_HIDDEN))
    w2 = cmp_w2.astype(BF16)
    cflag = jnp.stack([jnp.ones((1, HEAD_DIM), F32), jnp.zeros((1, HEAD_DIM), F32)])
    cgain = jnp.stack([g_k[0].astype(F32)[None, :], jnp.ones((1, HEAD_DIM), F32)])

    ident = jnp.arange(nbatch * t // PAGE, dtype=I32).reshape(nbatch, t // PAGE)
    pages_p = zp[:, HD:HD + 4 * KD].reshape(1, nbatch * t // PAGE, PAGE, 4 * NSA_KV_HEADS, HEAD_DIM)
    a_p = _cmp_a(pages_p, 0, ident, wcat)
    cmp_p = _cmp_b(a_p, c0, w2, cflag, cgain)
    gates_p = zp[:, g0:g0 + 3 * N_HEADS].reshape(nbatch * t, NSA_KV_HEADS, 3 * NSA_GROUP).transpose(1, 0, 2)
    op = _nsa_prompt_attn(zp, gates_p, cmp_p, bias_p, bias_cp, nbatch, t)
    xp = _oproj(op, wo, xp, tm=512)

    pages_s = cache.reshape(cache.shape[0], cache.shape[1], PAGE, 4 * NSA_KV_HEADS, HEAD_DIM)
    a_s = _cmp_a(pages_s, layer, page_table, wcat)
    cmp_s = _cmp_b(a_s, c0, w2, cflag, cgain)
    gates_s = zs[:, g0:g0 + 3 * N_HEADS].reshape(sbatch, s_len, N_HEADS, 3).transpose(0, 2, 1, 3)
    gates_s = gates_s.reshape(sbatch, N_HEADS * s_len, 3)
    win_new = zs[:, HD + 4 * KD:g0].reshape(sbatch, s_len, 2 * KD)
    w_all = jnp.concatenate([win_buf.reshape(sbatch, -1, 2 * KD), win_new], axis=1)
    n_w = w_all.shape[1]
    wall = jnp.pad(w_all, ((0, 0), (0, bias_w.shape[1] - n_w), (0, 0)))
    new_rows = jnp.pad(zs[:, HD + 2 * KD:HD + 4 * KD].reshape(sbatch, s_len, 2 * KD), ((0, 0), (0, PAGE - s_len), (0, 0)))
    osm = _nsa_sample_attn(zs, gates_s, cmp_s, bias_c, bias_s, bias_w, wall, new_rows, cache, layer, page_table,
                           s_len, past_len)
    xs = _oproj(osm, wo, xs, tm=xs.shape[0])

    rows_p = zp[:, HD:HD + 4 * KD].reshape(nbatch, t, 4, NSA_KV_HEADS, HEAD_DIM)
    rows_s = zs[:, HD:HD + 4 * KD].reshape(sbatch, s_len, 4, NSA_KV_HEADS, HEAD_DIM)
    keep = min(WINDOW, t)
    win_p = zp[:, HD + 4 * KD:g0].reshape(nbatch, t, 2, NSA_KV_HEADS, HEAD_DIM)[:, t - keep:]
    win_s = w_all[:, s_len:].reshape(sbatch, n_w - s_len, 2, NSA_KV_HEADS, HEAD_DIM)
    return xp, xs, rows_p, rows_s, win_p, win_s


def kernel(x_prompt, x_sample, cache_moba_kv, cache_nsa_kv, state_nsa_win, page_table, rel_bias, ffn_norm, ffn_w_gate, ffn_w_up, ffn_w_down, mix_norm, moba_w_qkv, moba_w_o, moba_q_norm, moba_k_norm, nsa_w_in, nsa_w_o, nsa_q_norm, nsa_k_norm, nsa_cmp_pos, nsa_cmp_w1, nsa_cmp_b1, nsa_cmp_w2):
    nbatch, t, _ = x_prompt.shape
    sbatch, s_len, _ = x_sample.shape
    past_len = page_table.shape[1] * PAGE
    wb = state_nsa_win.shape[2]
    assert t % TQ == 0 and past_len % TQ == 0 and wb == WINDOW and s_len <= 8
    xp = x_prompt.reshape(nbatch * t, D_MODEL)
    xs = x_sample.reshape(sbatch * s_len, D_MODEL)
    rows = N_HEADS * s_len

    bias_p = _bias_tiles(rel_bias, 3, TQ, TQ, 0, TQ)
    bias_s = _bias_tiles(rel_bias, 3, s_len, TQ, 0, TQ).reshape(3, rows, TQ)
    bias_cp = _bias_tiles(rel_bias, t // TQ, TQ, t // CMP_STRIDE, -(CMP_LEN - 1), TQ, CMP_STRIDE)
    n_cmp_s = past_len // CMP_STRIDE
    bias_c = _bias_tiles(rel_bias, 1, s_len, n_cmp_s, past_len - (CMP_LEN - 1), 0, CMP_STRIDE).reshape(rows, n_cmp_s)
    n_w = -(-(wb + s_len) // PAGE) * PAGE
    bias_w = _bias_tiles(rel_bias, 1, s_len, n_w, wb, 0).reshape(rows, n_w)

    moba_p, moba_s, nsa_p, nsa_s, win_p, win_s = [], [], [], [], [], []
    for i in range(DEPTH):
        j = i // 2
        xp, xs = _half_ffn(xp, xs, ffn_norm[i, 0], ffn_w_gate[i, 0], ffn_w_up[i, 0], ffn_w_down[i, 0])
        if i % 2 == 0:
            xp, xs, kv_p, kv_s = _moba_layer(xp, xs, nbatch, t, sbatch, s_len, mix_norm[i], moba_w_qkv[j], moba_w_o[j],
                                             moba_q_norm[j], moba_k_norm[j], cache_moba_kv, j, page_table, bias_p, bias_s)
            moba_p.append(kv_p)
            moba_s.append(kv_s)
        else:
            xp, xs, rp, rs, wp, ws = _nsa_layer(xp, xs, nbatch, t, sbatch, s_len, past_len, mix_norm[i], nsa_w_in[j],
                                                nsa_w_o[j], nsa_q_norm[j], nsa_k_norm[j], nsa_cmp_pos[j], nsa_cmp_w1[j],
                                                nsa_cmp_b1[j], nsa_cmp_w2[j], cache_nsa_kv, state_nsa_win[j], j,
                                                page_table, bias_p, bias_cp, bias_s, bias_c, bias_w)
            nsa_p.append(rp)
            nsa_s.append(rs)
            win_p.append(wp)
            win_s.append(ws)
        xp, xs = _half_ffn(xp, xs, ffn_norm[i, 1], ffn_w_gate[i, 1], ffn_w_up[i, 1], ffn_w_down[i, 1])
    return (xp.reshape(nbatch, t, D_MODEL), xs.reshape(sbatch, s_len, D_MODEL), jnp.stack(moba_p), jnp.stack(moba_s),
            jnp.stack(nsa_p), jnp.stack(nsa_s), jnp.stack(win_p), jnp.stack(win_s))
```

```python
import functools
import math

import jax
import jax.numpy as jnp
from jax import lax
from jax.experimental import pallas as pl
from jax.experimental.pallas import tpu as pltpu

F32 = jnp.float32
BF16 = jnp.bfloat16
I32 = jnp.int32
HIGHEST = lax.Precision.HIGHEST

D_MODEL = 2048
DEPTH = 4
HEAD_DIM = 128
N_HEADS = 16
HD = N_HEADS * HEAD_DIM
D_FF = 5504
D_FF_PAD = 5632
MOBA_BLOCK = 256
MOBA_TOPK = 3
NSA_KV_HEADS = 4
NSA_GROUP = 4
KD = NSA_KV_HEADS * HEAD_DIM
CMP_LEN = 32
CMP_STRIDE = 16
CMP_HIDDEN = 256
SEL_BLOCK = 64
SEL_TOPN = 16
WINDOW = 512
FORCE_BONUS = 1e4
N_BUCKETS = 32
MAX_DISTANCE = 128
RMS_EPS = 1e-6
ATTN_SCALE = HEAD_DIM ** -0.5
PAGE = 128
TQ = 256
NSA_IN = HD + 6 * KD + 3 * N_HEADS
NSA_IN_PAD = 5376
NEG = -1e30
VMEM_LIMIT = 56 * 1024 * 1024

NT = (((1,), (1,)), ((), ()))


def _params(sem, vmem=VMEM_LIMIT):
    return pltpu.CompilerParams(dimension_semantics=sem, vmem_limit_bytes=vmem)


def _dot(a, b, **kw):
    return jnp.dot(a, b, preferred_element_type=F32, **kw)


def _dot_nt(a, b, **kw):
    return lax.dot_general(a, b, NT, preferred_element_type=F32, **kw)


def _iota(shape, dim):
    return lax.broadcasted_iota(I32, shape, dim)


def _rms_scale(x):
    return lax.rsqrt(jnp.mean(x * x, axis=-1, keepdims=True) + RMS_EPS)


def _t5_bucket(dist):
    n = jnp.maximum(dist, 0)
    exact = N_BUCKETS // 2
    nf = jnp.maximum(n, 1).astype(F32)
    log_b = exact + (jnp.log(nf * (1.0 / exact)) / math.log(MAX_DISTANCE / exact) * (N_BUCKETS - exact)).astype(I32)
    return jnp.where(n < exact, n, jnp.minimum(log_b, N_BUCKETS - 1))


def _bias_lookup(bucket, tab_ref, h):
    acc = jnp.zeros(bucket.shape, F32)
    for b in range(N_BUCKETS):
        acc = jnp.where(bucket == b, tab_ref[b, h], acc)
    return acc


def _topk_mask(score, idx, k):
    rank = jnp.zeros(score.shape, I32)
    for m in range(score.shape[1]):
        c = score[:, m:m + 1]
        beats = (c > score) | ((c == score) & (idx > m))
        rank = rank + beats.astype(I32)
    return rank < k


def _pick_col(mat, idx, j):
    return jnp.sum(jnp.where(idx == j, mat, 0.0), axis=1, keepdims=True)


def _softmax_first(s, mask):
    s = jnp.where(mask, s, NEG)
    m = jnp.max(s, axis=1, keepdims=True)
    p = jnp.where(mask, jnp.exp(s - m), 0.0)
    return m, jnp.sum(p, axis=1, keepdims=True), p


def _softmax_next(s, mask, m, l):
    s = jnp.where(mask, s, NEG)
    m_new = jnp.maximum(m, jnp.max(s, axis=1, keepdims=True))
    a = jnp.exp(m - m_new)
    p = jnp.where(mask, jnp.exp(s - m_new), 0.0)
    return m_new, a * l + jnp.sum(p, axis=1, keepdims=True), p, a


def _ffn_kernel(x_ref, g_ref, wg_ref, wu_ref, wd_ref, o_ref, h_ref, acc_ref):
    j = pl.program_id(1)

    @pl.when(j == 0)
    def _():
        x = x_ref[...]
        h_ref[...] = (x * _rms_scale(x) * g_ref[...]).astype(BF16)
        acc_ref[...] = jnp.zeros_like(acc_ref)

    h = h_ref[...]
    a = _dot(h, wg_ref[...])
    b = _dot(h, wu_ref[...])
    t = (a * jax.nn.sigmoid(a) * b).astype(BF16)
    acc_ref[...] += _dot(t, wd_ref[...])

    @pl.when(j == pl.num_programs(1) - 1)
    def _():
        o_ref[...] = x_ref[...] + 0.5 * acc_ref[...]


def _ffn(x, g, wg, wu, wd, tm, tf=512):
    n = x.shape[0]
    fp = wg.shape[1]
    return pl.pallas_call(
        _ffn_kernel,
        out_shape=jax.ShapeDtypeStruct((n, D_MODEL), F32),
        grid=(n // tm, fp // tf),
        in_specs=[
            pl.BlockSpec((tm, D_MODEL), lambda i, j: (i, 0)),
            pl.BlockSpec((1, D_MODEL), lambda i, j: (0, 0)),
            pl.BlockSpec((D_MODEL, tf), lambda i, j: (0, j)),
            pl.BlockSpec((D_MODEL, tf), lambda i, j: (0, j)),
            pl.BlockSpec((tf, D_MODEL), lambda i, j: (j, 0)),
        ],
        out_specs=pl.BlockSpec((tm, D_MODEL), lambda i, j: (i, 0)),
        scratch_shapes=[pltpu.VMEM((tm, D_MODEL), BF16), pltpu.VMEM((tm, D_MODEL), F32)],
        compiler_params=_params(("parallel", "arbitrary")),
        name="ffn",
    )(x, g.reshape(1, D_MODEL), wg, wu, wd)


def _proj_kernel(x_ref, g_ref, w_ref, f_ref, gn_ref, o_ref, h_ref):
    @pl.when(pl.program_id(1) == 0)
    def _():
        x = x_ref[...]
        h_ref[...] = (x * _rms_scale(x) * g_ref[...]).astype(BF16)

    y = _dot(h_ref[...], w_ref[...])
    f = f_ref[...]
    gn = gn_ref[...]
    for c in range(y.shape[1] // HEAD_DIM):
        sl = slice(c * HEAD_DIM, (c + 1) * HEAD_DIM)
        yc = y[:, sl]
        sc = f[:, sl] * _rms_scale(yc) + (1.0 - f[:, sl])
        o_ref[:, sl] = yc * sc * gn[:, sl]


def _proj(x, g, w, flag, gain, tm, tn):
    n = x.shape[0]
    npad = w.shape[1]
    return pl.pallas_call(
        _proj_kernel,
        out_shape=jax.ShapeDtypeStruct((n, npad), F32),
        grid=(n // tm, npad // tn),
        in_specs=[
            pl.BlockSpec((tm, D_MODEL), lambda i, j: (i, 0)),
            pl.BlockSpec((1, D_MODEL), lambda i, j: (0, 0)),
            pl.BlockSpec((D_MODEL, tn), lambda i, j: (0, j)),
            pl.BlockSpec((1, tn), lambda i, j: (0, j)),
            pl.BlockSpec((1, tn), lambda i, j: (0, j)),
        ],
        out_specs=pl.BlockSpec((tm, tn), lambda i, j: (i, j)),
        scratch_shapes=[pltpu.VMEM((tm, D_MODEL), BF16)],
        compiler_params=_params(("parallel", "arbitrary")),
        name="proj",
    )(x, g.reshape(1, D_MODEL), w, flag, gain)


def _oproj_kernel(a_ref, w_ref, x_ref, o_ref):
    o_ref[...] = x_ref[...] + _dot(a_ref[...].astype(BF16), w_ref[...])


def _oproj(a, w, x, tm, tn=512):
    n = a.shape[0]
    return pl.pallas_call(
        _oproj_kernel,
        out_shape=jax.ShapeDtypeStruct((n, D_MODEL), F32),
        grid=(n // tm, D_MODEL // tn),
        in_specs=[
            pl.BlockSpec((tm, HD), lambda i, j: (i, 0)),
            pl.BlockSpec((HD, tn), lambda i, j: (0, j)),
            pl.BlockSpec((tm, tn), lambda i, j: (i, j)),
        ],
        out_specs=pl.BlockSpec((tm, tn), lambda i, j: (i, j)),
        compiler_params=_params(("parallel", "arbitrary")),
        name="oproj",
    )(a, w, x)


def _bias_kernel(tab_ref, o_ref, *, base, stride, cmul):
    h = pl.program_id(0)
    nk, rows, cols = o_ref.shape
    d0 = _iota((rows, cols), 0) - cmul * _iota((rows, cols), 1) + base
    for kind in range(nk):
        d = d0 + kind * stride
        o_ref[kind] = jnp.where(d < 0, NEG, _bias_lookup(_t5_bucket(d), tab_ref, h))


def _bias_tiles(rel_bias, nk, rows, cols, base, stride, cmul=1):
    return pl.pallas_call(
        functools.partial(_bias_kernel, base=base, stride=stride, cmul=cmul),
        out_shape=jax.ShapeDtypeStruct((nk, N_HEADS, rows, cols), F32),
        grid=(N_HEADS,),
        in_specs=[pl.BlockSpec(memory_space=pltpu.SMEM)],
        out_specs=pl.BlockSpec((nk, None, rows, cols), lambda h: (0, h, 0, 0)),
        compiler_params=_params(("arbitrary",)),
        name="bias_tiles",
    )(rel_bias)


def _kmean_kernel(pt_ref, k0_ref, k1_ref, o_ref):
    s = jnp.sum(k0_ref[...], axis=0, keepdims=True) + jnp.sum(k1_ref[...], axis=0, keepdims=True)
    o_ref[...] = s * (1.0 / MOBA_BLOCK)


def _kmean(pages, layer, kcol, page_table):
    nb_, npg = page_table.shape
    nblk = npg // 2
    spec = lambda e: pl.BlockSpec((None, None, PAGE, HD), lambda b, n, pt: (layer, pt[b, 2 * n + e], 0, kcol))
    out = pl.pallas_call(
        _kmean_kernel,
        out_shape=jax.ShapeDtypeStruct((nb_, nblk, 1, HD), F32),
        grid_spec=pltpu.PrefetchScalarGridSpec(
            num_scalar_prefetch=1, grid=(nb_, nblk),
            in_specs=[spec(0), spec(1)],
            out_specs=pl.BlockSpec((None, None, 1, HD), lambda b, n, pt: (b, n, 0, 0))),
        compiler_params=_params(("parallel", "arbitrary")),
        name="kmean",
    )(page_table, pages, pages)
    return out.reshape(nb_, nblk, HD)


KMEAN_BLOCKS = 2


def _kmean_cache_kernel(pt_ref, *refs):
    o_ref = refs[-1]
    for e in range(KMEAN_BLOCKS):
        o_ref[e] = (jnp.sum(refs[2 * e][...], axis=0) + jnp.sum(refs[2 * e + 1][...], axis=0)) * (1.0 / MOBA_BLOCK)


def _kmean_cache(cache, layer, page_table):
    nb_, npg = page_table.shape
    nblk = npg // 2
    per = 2 * KMEAN_BLOCKS
    assert nblk % KMEAN_BLOCKS == 0
    rows = cache.reshape(cache.shape[0], cache.shape[1], PAGE, 2 * N_HEADS, HEAD_DIM)
    spec = lambda e: pl.BlockSpec((None, None, PAGE, N_HEADS, HEAD_DIM),
                                  lambda b, n, pt: (layer, pt[b, per * n + e], 0, 0, 0))
    out = pl.pallas_call(
        _kmean_cache_kernel,
        out_shape=jax.ShapeDtypeStruct((nb_, nblk, N_HEADS, HEAD_DIM), F32),
        grid_spec=pltpu.PrefetchScalarGridSpec(
            num_scalar_prefetch=1, grid=(nb_, nblk // KMEAN_BLOCKS),
            in_specs=[spec(e) for e in range(per)],
            out_specs=pl.BlockSpec((None, KMEAN_BLOCKS, N_HEADS, HEAD_DIM), lambda b, n, pt: (b, n, 0, 0))),
        compiler_params=_params(("parallel", "arbitrary")),
        name="kmean_cache",
    )(page_table, *([rows] * per))
    return out.reshape(nb_, nblk, HD)


MOBA_HPS = 4


def _moba_prompt_kernel(q_ref, k_ref, v_ref, km_ref, bias_ref, o_ref):
    i = pl.program_id(2)
    nb = km_ref.shape[0]
    jb = _iota((nb, TQ), 0)
    past = jb < i
    lane = _iota((TQ, HEAD_DIM), 1)
    heads = [slice(e * HEAD_DIM, (e + 1) * HEAD_DIM) for e in range(MOBA_HPS)]

    q_aug = []
    for sl in heads:
        q = q_ref[:, sl]
        gate = jnp.where(past, _dot_nt(km_ref[:, sl], q, precision=HIGHEST), -jnp.inf)
        rank = jnp.zeros((nb, TQ), I32)
        for m in range(nb):
            c = gate[m:m + 1, :]
            rank = rank + ((c > gate) | ((c == gate) & (jb > m))).astype(I32)
        pen = jnp.where((rank < MOBA_TOPK) & past, 0.0, NEG)
        pen = jnp.concatenate([pen, jnp.zeros((HEAD_DIM - nb, TQ), F32)], axis=0).T
        q_aug.append(jnp.concatenate([(q * ATTN_SCALE).astype(BF16), pen.astype(BF16)], axis=1))

    def scores(j, hot, kind):
        rows = pl.ds(pl.multiple_of(j * TQ, TQ), TQ)
        parts = [_dot_nt(q_aug[e], jnp.concatenate([k_ref[rows, heads[e]].astype(BF16), hot], axis=1))
                 for e in range(MOBA_HPS)]
        return jnp.concatenate(parts, axis=0) + bias_ref[kind].reshape(MOBA_HPS * TQ, TQ)

    def weighted(p, j):
        rows = pl.ds(pl.multiple_of(j * TQ, TQ), TQ)
        return jnp.concatenate([_dot(p[e * TQ:(e + 1) * TQ], v_ref[rows, heads[e]].astype(BF16))
                                for e in range(MOBA_HPS)], axis=0)

    s = scores(i, jnp.zeros((TQ, HEAD_DIM), BF16), 0)
    m = jnp.max(s, axis=1, keepdims=True)
    p = jnp.exp(s - m)
    carry = (m, jnp.sum(p, axis=1, keepdims=True), weighted(p.astype(BF16), i))

    def body(j, carry):
        m, l, acc = carry
        s = scores(j, jnp.where(lane == j, 1.0, 0.0).astype(BF16), jnp.minimum(i - j, 2))
        m_new = jnp.maximum(m, jnp.max(s, axis=1, keepdims=True))
        a = jnp.exp(m - m_new)
        p = jnp.exp(s - m_new)
        return m_new, a * l + jnp.sum(p, axis=1, keepdims=True), a * acc + weighted(p.astype(BF16), j)

    _, l, acc = lax.fori_loop(0, i, body, carry)
    o = acc / l
    for e in range(MOBA_HPS):
        o_ref[:, heads[e]] = o[e * TQ:(e + 1) * TQ]


def _moba_prompt_attn(y, kmean, bias, nbatch, t):
    nq = t // TQ
    assert nq <= HEAD_DIM
    w = MOBA_HPS * HEAD_DIM
    ngrp = N_HEADS // MOBA_HPS
    return pl.pallas_call(
        _moba_prompt_kernel,
        out_shape=jax.ShapeDtypeStruct((nbatch * t, HD), F32),
        grid=(nbatch, ngrp, nq),
        in_specs=[
            pl.BlockSpec((TQ, w), lambda b, h, i: (b * nq + i, h)),
            pl.BlockSpec((t, w), lambda b, h, i: (b, ngrp + h)),
            pl.BlockSpec((t, w), lambda b, h, i: (b, 2 * ngrp + h)),
            pl.BlockSpec((None, nq, w), lambda b, h, i: (b, 0, h)),
            pl.BlockSpec((3, MOBA_HPS, TQ, TQ), lambda b, h, i: (0, h, 0, 0)),
        ],
        out_specs=pl.BlockSpec((TQ, w), lambda b, h, i: (b * nq + i, h)),
        compiler_params=_params(("parallel", "parallel", "arbitrary")),
        name="moba_prompt_attn",
    )(y, y, y, kmean, bias)


def _gather_heads(r0, r1, h0, nh):
    slots = r0.shape[0] // PAGE
    return jnp.concatenate(
        [jnp.concatenate([r[pl.ds(h0 + h, PAGE, stride=slots), :] for r in (r0, r1)], axis=0) for h in range(nh)],
        axis=1)


def _moba_sample_kernel(pt_ref, q_ref, km_ref, bias_ref, knew_ref, vnew_ref, pg0_ref, pg1_ref, o_ref,
                        qb_ref, sel_ref, m_ref, l_ref, acc_ref, stage_ref, *, s_len):
    p = pl.program_id(1)
    nblk = km_ref.shape[0]
    rows = N_HEADS * s_len
    nidx = _iota((rows, nblk), 1)

    def update(first, k, v, bias, mask):
        s = _dot_nt(qb_ref[...], k.astype(BF16)) * ATTN_SCALE + bias
        if first:
            m, l, pr = _softmax_first(s, mask)
            acc_ref[...] = _dot(pr.astype(BF16), v.astype(BF16))
        else:
            m, l, pr, a = _softmax_next(s, mask, m_ref[...], l_ref[...])
            acc_ref[...] = a * acc_ref[...] + _dot(pr.astype(BF16), v.astype(BF16))
        m_ref[...] = m
        l_ref[...] = l

    @pl.when(p == 0)
    def _():
        q = q_ref[...]
        blocks = []
        for h in range(N_HEADS):
            sl = slice(h * HEAD_DIM, (h + 1) * HEAD_DIM)
            lane_ok = (_iota((s_len, HD), 1) // HEAD_DIM) == h
            blocks.append(jnp.where(lane_ok, jnp.concatenate([q[:, sl]] * N_HEADS, axis=1), 0.0))
        qbd = jnp.concatenate(blocks, axis=0)
        gate = _dot_nt(qbd, km_ref[...], precision=HIGHEST)
        sel_ref[...] = _topk_mask(gate, nidx, MOBA_TOPK).astype(F32)
        qb_ref[...] = qbd.astype(BF16)
        t = _iota((rows, PAGE), 0) % s_len
        update(True, knew_ref[...], vnew_ref[...], bias_ref[0, :, 0:PAGE], _iota((rows, PAGE), 1) <= t)

    @pl.when(p > 0)
    def _():
        n = p - 1
        picked = _pick_col(sel_ref[...], nidx, n) > 0.5
        mask = jnp.broadcast_to(picked, (rows, MOBA_BLOCK))
        slots = 2 * N_HEADS
        for e, pg in enumerate((pg0_ref, pg1_ref)):
            for a in range(4):
                stage_ref[e, a] = pg[pl.ds(a, PAGE * slots // 4, stride=4), :]

        def head_rows(slot):
            return jnp.concatenate(
                [stage_ref[e, slot % 4, pl.ds(slot // 4, PAGE, stride=slots // 4), :] for e in range(2)], axis=0)

        k = jnp.concatenate([head_rows(h) for h in range(N_HEADS)], axis=1)
        v = jnp.concatenate([head_rows(N_HEADS + h) for h in range(N_HEADS)], axis=1)
        update(False, k, v, bias_ref[jnp.minimum(nblk - n, 2)], mask)

    @pl.when(p == nblk)
    def _():
        inv = 1.0 / l_ref[...]
        for h in range(N_HEADS):
            sl = slice(h * HEAD_DIM, (h + 1) * HEAD_DIM)
            rs = slice(h * s_len, (h + 1) * s_len)
            o_ref[:, sl] = acc_ref[rs, sl] * inv[rs]


def _moba_sample_attn(y, kmean, bias, kv_new, cache, layer, page_table, s_len):
    nb_, n_pages = page_table.shape
    nblk = n_pages // 2
    rows = N_HEADS * s_len
    slots = 2 * N_HEADS
    cache5 = cache.reshape(cache.shape[0], cache.shape[1], PAGE * slots, HEAD_DIM)
    pg_spec = lambda e: pl.BlockSpec(
        (None, None, PAGE * slots, HEAD_DIM), lambda b, p, pt: (layer, pt[b, 2 * jnp.maximum(p - 1, 0) + e], 0, 0))
    return pl.pallas_call(
        functools.partial(_moba_sample_kernel, s_len=s_len),
        out_shape=jax.ShapeDtypeStruct((nb_ * s_len, HD), F32),
        grid_spec=pltpu.PrefetchScalarGridSpec(
            num_scalar_prefetch=1, grid=(nb_, nblk + 1),
            in_specs=[
                pl.BlockSpec((s_len, HD), lambda b, p, pt: (b, 0)),
                pl.BlockSpec((None, nblk, HD), lambda b, p, pt: (b, 0, 0)),
                pl.BlockSpec((3, rows, MOBA_BLOCK), lambda b, p, pt: (0, 0, 0)),
                pl.BlockSpec((None, PAGE, HD), lambda b, p, pt: (b, 0, 0)),
                pl.BlockSpec((None, PAGE, HD), lambda b, p, pt: (b, 0, 1)),
                pg_spec(0), pg_spec(1),
            ],
            out_specs=pl.BlockSpec((s_len, HD), lambda b, p, pt: (b, 0)),
            scratch_shapes=[
                pltpu.VMEM((rows, HD), BF16), pltpu.VMEM((rows, nblk), F32),
                pltpu.VMEM((rows, 1), F32), pltpu.VMEM((rows, 1), F32), pltpu.VMEM((rows, HD), F32),
                pltpu.VMEM((2, 4, PAGE * slots // 4, HEAD_DIM), F32)]),
        compiler_params=_params(("parallel", "arbitrary")),
        name="moba_sample_attn",
    )(page_table, y, kmean, bias, kv_new, kv_new, cache5, cache5)


def _cmp_a_kernel(pt_ref, *refs, pg):
    x_refs, w_ref, o_ref, lhs_ref = refs[:pg], refs[pg], refs[pg + 1], refs[pg + 2]
    n_str = PAGE // CMP_STRIDE
    slots = 4 * NSA_KV_HEADS
    for kind in range(2):
        for c in range(pg):
            x = x_refs[c]
            for k in range(NSA_KV_HEADS):
                r0 = (k * pg + c) * n_str
                slot = kind * NSA_KV_HEADS + k
                for l in range(CMP_STRIDE):
                    lhs_ref[kind, r0:r0 + n_str, l * HEAD_DIM:(l + 1) * HEAD_DIM] = (
                        x[pl.ds(l * slots + slot, n_str, stride=CMP_STRIDE * slots), :])
        res = _dot(lhs_ref[kind].astype(BF16), w_ref[kind])
        for k in range(NSA_KV_HEADS):
            o_ref[kind, k] = res[k * pg * n_str:(k + 1) * pg * n_str]


def _cmp_a(pages, layer, page_table, wcat, pg=8):
    nb_, npg = page_table.shape
    n_str = PAGE // CMP_STRIDE
    spec = lambda c: pl.BlockSpec((None, None, PAGE * 4 * NSA_KV_HEADS, HEAD_DIM),
                                  lambda b, s, pt: (layer, pt[b, s * pg + c], 0, 0))
    return pl.pallas_call(
        functools.partial(_cmp_a_kernel, pg=pg),
        out_shape=jax.ShapeDtypeStruct((nb_, 2, NSA_KV_HEADS, npg * n_str, 2 * CMP_HIDDEN), F32),
        grid_spec=pltpu.PrefetchScalarGridSpec(
            num_scalar_prefetch=1, grid=(nb_, npg // pg),
            in_specs=[spec(c) for c in range(pg)] + [
                pl.BlockSpec((2, CMP_STRIDE * HEAD_DIM, 2 * CMP_HIDDEN), lambda b, s, pt: (0, 0, 0))],
            out_specs=pl.BlockSpec((None, 2, NSA_KV_HEADS, pg * n_str, 2 * CMP_HIDDEN),
                                   lambda b, s, pt: (b, 0, 0, s, 0)),
            scratch_shapes=[pltpu.VMEM((2, NSA_KV_HEADS * pg * n_str, CMP_STRIDE * HEAD_DIM), F32)]),
        compiler_params=_params(("parallel", "arbitrary")),
        name="cmp_a",
    )(page_table, *([pages] * pg), wcat)


def _c0_kernel(p_ref, w_ref, b_ref, o_ref):
    o_ref[...] = _dot(p_ref[...], w_ref[...]) + b_ref[...]


def _cmp_c0(pos_flat, w1, b1):
    return pl.pallas_call(
        _c0_kernel,
        out_shape=jax.ShapeDtypeStruct((2, 8, CMP_HIDDEN), F32),
        grid=(2,),
        in_specs=[
            pl.BlockSpec((None, 8, CMP_LEN * HEAD_DIM), lambda i: (i, 0, 0)),
            pl.BlockSpec((None, CMP_LEN * HEAD_DIM, CMP_HIDDEN), lambda i: (i, 0, 0)),
            pl.BlockSpec((None, 1, CMP_HIDDEN), lambda i: (i, 0, 0)),
        ],
        out_specs=pl.BlockSpec((None, 8, CMP_HIDDEN), lambda i: (i, 0, 0)),
        compiler_params=_params(("arbitrary",)),
        name="cmp_c0",
    )(pos_flat, w1, b1)


def _cmp_b_kernel(a_ref, c0_ref, w2_ref, f_ref, g_ref, o_ref):
    n = a_ref.shape[0]
    a0 = a_ref[:, 0:CMP_HIDDEN]
    a1 = pltpu.roll(a_ref[:, CMP_HIDDEN:2 * CMP_HIDDEN], n - 1, 0)
    h1 = a0 + a1 + c0_ref[0:1, :]
    y = _dot(jax.nn.gelu(h1).astype(BF16), w2_ref[...])
    f = f_ref[...]
    o_ref[...] = y * (f * _rms_scale(y) + (1.0 - f)) * g_ref[...]


def _cmp_b(a, c0, w2, flag, gain):
    nb_, _, _, n_str, _ = a.shape
    return pl.pallas_call(
        _cmp_b_kernel,
        out_shape=jax.ShapeDtypeStruct((nb_, 2, n_str, KD), F32),
        grid=(nb_, 2, NSA_KV_HEADS),
        in_specs=[
            pl.BlockSpec((None, None, None, n_str, 2 * CMP_HIDDEN), lambda b, c, k: (b, c, k, 0, 0)),
            pl.BlockSpec((None, 8, CMP_HIDDEN), lambda b, c, k: (c, 0, 0)),
            pl.BlockSpec((None, CMP_HIDDEN, HEAD_DIM), lambda b, c, k: (c, 0, 0)),
            pl.BlockSpec((None, 1, HEAD_DIM), lambda b, c, k: (c, 0, 0)),
            pl.BlockSpec((None, 1, HEAD_DIM), lambda b, c, k: (c, 0, 0)),
        ],
        out_specs=pl.BlockSpec((None, None, n_str, HEAD_DIM), lambda b, c, k: (b, c, 0, k)),
        compiler_params=_params(("parallel", "arbitrary", "arbitrary")),
        name="cmp_b",
    )(a, c0, w2, flag, gain)


def _stride_to_block_matrix(n_cmp, n_blk, transpose):
    shape = (n_blk, n_cmp) if transpose else (n_cmp, n_blk)
    n = _iota(shape, 1 if transpose else 0)
    j = _iota(shape, 0 if transpose else 1)
    r = n - (SEL_BLOCK // CMP_STRIDE) * j
    return jnp.where((r == -1) | (r == 3), 1.0, jnp.where((r >= 0) & (r <= 2), 2.0, 0.0)).astype(F32)


def _nsa_prompt_kernel(q_ref, g_ref, kc_ref, vc_ref, ks_ref, vs_ref, kw_ref, vw_ref, bias_ref, bc_ref, o_ref):
    i = pl.program_id(2)
    q0 = i * TQ
    G = NSA_GROUP
    n_cmp = kc_ref.shape[0]
    n_blk = ks_ref.shape[0] // SEL_BLOCK
    qg = jnp.concatenate([q_ref[:, g * HEAD_DIM:(g + 1) * HEAD_DIM] for g in range(G)], axis=0).astype(BF16)

    tq = q0 + _iota((TQ, n_cmp), 0)
    dist = tq - (CMP_STRIDE * _iota((TQ, n_cmp), 1) + CMP_LEN - 1)
    vis = jnp.concatenate([dist >= 0] * G, axis=0)
    sc = _dot_nt(qg, kc_ref[...].astype(BF16)) * ATTN_SCALE + bc_ref[...].reshape(G * TQ, n_cmp)
    _, lc, pc = _softmax_first(sc, vis)
    pc = pc / jnp.maximum(lc, 1e-30)
    oc = _dot(pc.astype(BF16), vc_ref[...].astype(BF16))

    pcs = pc[0:TQ]
    for g in range(1, G):
        pcs = pcs + pc[g * TQ:(g + 1) * TQ]
    score = _dot_nt(_stride_to_block_matrix(n_cmp, n_blk, True), pcs, precision=HIGHEST)
    jb = _iota((n_blk, TQ), 0)
    cur = (q0 + _iota((n_blk, TQ), 1)) // SEL_BLOCK
    forced = (jb == 0) | (jb == cur) | (jb == cur - 1)
    score = jnp.where(jb <= cur, score + jnp.where(forced, FORCE_BONUS, 0.0), -jnp.inf)
    rank = jnp.zeros((n_blk, TQ), I32)
    for m in range(n_blk):
        c = score[m:m + 1, :]
        rank = rank + ((c > score) | ((c == score) & (jb > m))).astype(I32)
    sel = ((rank < SEL_TOPN) & (jb <= cur)).astype(F32).T.astype(BF16)

    row = _iota((TQ, TQ), 0)
    col = _iota((TQ, TQ), 1)

    def tile(k_ref, v_ref, j, kind):
        sl = pl.ds(pl.multiple_of(j * TQ, TQ), TQ)
        s = _dot_nt(qg, k_ref[sl, :].astype(BF16)) * ATTN_SCALE + bias_ref[kind].reshape(G * TQ, TQ)
        return s, v_ref[sl, :].astype(BF16)

    def first(k_ref, v_ref, mask):
        s, vb = tile(k_ref, v_ref, i, 0)
        s = jnp.where(jnp.concatenate([mask] * G, axis=0), s, NEG)
        m = jnp.max(s, axis=1, keepdims=True)
        p = jnp.exp(s - m)
        return m, jnp.sum(p, axis=1, keepdims=True), _dot(p.astype(BF16), vb)

    def step(k_ref, v_ref, j, mask, carry):
        m, l, acc = carry
        s, vb = tile(k_ref, v_ref, j, jnp.minimum(i - j, 2))
        s = jnp.where(jnp.concatenate([mask] * G, axis=0), s, NEG)
        m_new = jnp.maximum(m, jnp.max(s, axis=1, keepdims=True))
        a = jnp.exp(m - m_new)
        p = jnp.exp(s - m_new)
        return m_new, a * l + jnp.sum(p, axis=1, keepdims=True), a * acc + _dot(p.astype(BF16), vb)

    def picked(j):
        e = (_iota((n_blk, TQ), 0) == j * (TQ // SEL_BLOCK) + _iota((n_blk, TQ), 1) // SEL_BLOCK)
        return _dot(sel, e.astype(BF16)) > 0.5

    carry = first(ks_ref, vs_ref, picked(i) & (col <= row))
    ms, ls, accs = lax.fori_loop(0, i, lambda j, c: step(ks_ref, vs_ref, j, picked(j), c), carry)
    osel = accs / ls

    carry = first(kw_ref, vw_ref, col <= row)

    def wbody(d, c):
        mask = (col - row) > jnp.where(d == 1, -TQ, 0)
        return step(kw_ref, vw_ref, i - d, mask, c)

    mw, lw, accw = lax.fori_loop(1, jnp.minimum(i, WINDOW // TQ) + 1, wbody, carry)
    ow = accw / lw

    gs = jax.nn.sigmoid(g_ref[...])
    for g in range(G):
        rs = slice(g * TQ, (g + 1) * TQ)
        o_ref[:, g * HEAD_DIM:(g + 1) * HEAD_DIM] = (
            gs[:, 3 * g:3 * g + 1] * oc[rs] + gs[:, 3 * g + 1:3 * g + 2] * osel[rs] + gs[:, 3 * g + 2:3 * g + 3] * ow[rs])


def _nsa_prompt_attn(z, gates, cmp, bias, bias_c, nbatch, t):
    nq = t // TQ
    n_cmp = t // CMP_STRIDE
    kv0 = HD // HEAD_DIM
    kv_spec = lambda kind: pl.BlockSpec((t, HEAD_DIM), lambda b, k, i: (b, kv0 + kind * NSA_KV_HEADS + k))
    return pl.pallas_call(
        _nsa_prompt_kernel,
        out_shape=jax.ShapeDtypeStruct((nbatch * t, HD), F32),
        grid=(nbatch, NSA_KV_HEADS, nq),
        in_specs=[
            pl.BlockSpec((TQ, NSA_GROUP * HEAD_DIM), lambda b, k, i: (b * nq + i, k)),
            pl.BlockSpec((None, TQ, 3 * NSA_GROUP), lambda b, k, i: (k, b * nq + i, 0)),
            pl.BlockSpec((None, None, n_cmp, HEAD_DIM), lambda b, k, i: (b, 0, 0, k)),
            pl.BlockSpec((None, None, n_cmp, HEAD_DIM), lambda b, k, i: (b, 1, 0, k)),
            kv_spec(2), kv_spec(3), kv_spec(4), kv_spec(5),
            pl.BlockSpec((3, NSA_GROUP, TQ, TQ), lambda b, k, i: (0, k, 0, 0)),
            pl.BlockSpec((None, NSA_GROUP, TQ, n_cmp), lambda b, k, i: (i, k, 0, 0)),
        ],
        out_specs=pl.BlockSpec((TQ, NSA_GROUP * HEAD_DIM), lambda b, k, i: (b * nq + i, k)),
        compiler_params=_params(("parallel", "parallel", "arbitrary")),
        name="nsa_prompt_attn",
    )(z, gates, cmp, cmp, z, z, z, z, bias, bias_c)


def _nsa_sample_kernel(pt_ref, q_ref, g_ref, cmp_ref, bc_ref, bs_ref, bw_ref, wall_ref, new_ref, pg0_ref, pg1_ref,
                       o_ref, qb_ref, sel_ref, m_ref, l_ref, acc_ref, ocw_ref, *, n_steps, s_len, past_len):
    p = pl.program_id(1)
    rows = N_HEADS * s_len
    n_cmp = cmp_ref.shape[1]
    n_sel = sel_ref.shape[1]
    jb = _iota((rows, n_sel), 1)

    def update(first, k, v, bias, mask):
        s = _dot_nt(qb_ref[...], k.astype(BF16)) * ATTN_SCALE + bias
        if first:
            m, l, pr = _softmax_first(s, mask)
            acc_ref[...] = _dot(pr.astype(BF16), v.astype(BF16))
        else:
            m, l, pr, a = _softmax_next(s, mask, m_ref[...], l_ref[...])
            acc_ref[...] = a * acc_ref[...] + _dot(pr.astype(BF16), v.astype(BF16))
        m_ref[...] = m
        l_ref[...] = l

    def picked(blk0, n_keys):
        e = (_iota((n_sel, n_keys), 0) == blk0 + _iota((n_sel, n_keys), 1) // SEL_BLOCK)
        return _dot(sel_ref[...].astype(BF16), e.astype(BF16)) > 0.5

    @pl.when(p == 0)
    def _():
        q = q_ref[...]
        blocks = []
        for h in range(N_HEADS):
            lane_ok = (_iota((s_len, KD), 1) // HEAD_DIM) == (h // NSA_GROUP)
            qh = q[:, h * HEAD_DIM:(h + 1) * HEAD_DIM]
            blocks.append(jnp.where(lane_ok, jnp.concatenate([qh] * NSA_KV_HEADS, axis=1), 0.0))
        qb = jnp.concatenate(blocks, axis=0).astype(BF16)
        qb_ref[...] = qb
        gs = jax.nn.sigmoid(g_ref[...])

        t_c = _iota((rows, n_cmp), 0) % s_len
        vis = CMP_STRIDE * _iota((rows, n_cmp), 1) + (CMP_LEN - 1) <= past_len + t_c
        sc = _dot_nt(qb, cmp_ref[0].astype(BF16)) * ATTN_SCALE + bc_ref[...]
        _, lc, pc = _softmax_first(sc, vis)
        pc = pc / jnp.maximum(lc, 1e-30)
        oc = _dot(pc.astype(BF16), cmp_ref[1].astype(BF16))

        r = _iota((rows, rows), 0)
        c = _iota((rows, rows), 1)
        same = ((r // (NSA_GROUP * s_len)) == (c // (NSA_GROUP * s_len))) & ((r % s_len) == (c % s_len))
        pcs = _dot(same.astype(F32), pc, precision=HIGHEST)
        score = _dot(pcs, _stride_to_block_matrix(n_cmp, n_sel, False), precision=HIGHEST)
        cur = (past_len + _iota((rows, n_sel), 0) % s_len) // SEL_BLOCK
        forced = (jb == 0) | (jb == cur) | (jb == cur - 1)
        score = jnp.where(jb <= cur, score + jnp.where(forced, FORCE_BONUS, 0.0), -jnp.inf)
        sel_ref[...] = (_topk_mask(score, jb, SEL_TOPN) & (jb <= cur)).astype(F32)

        n_w = wall_ref.shape[0]
        dw = WINDOW + _iota((rows, n_w), 0) % s_len - _iota((rows, n_w), 1)
        sw = _dot_nt(qb, wall_ref[:, 0:KD].astype(BF16)) * ATTN_SCALE + bw_ref[...]
        _, lw, pw = _softmax_first(sw, (dw >= 0) & (dw < WINDOW))
        pw = pw / jnp.maximum(lw, 1e-30)
        ow = _dot(pw.astype(BF16), wall_ref[:, KD:2 * KD].astype(BF16))
        ocw_ref[...] = gs[:, 0:1] * oc + gs[:, 2:3] * ow

        t_s = _iota((rows, PAGE), 0) % s_len
        mask = picked(past_len // SEL_BLOCK, PAGE) & (_iota((rows, PAGE), 1) <= t_s)
        update(True, new_ref[:, 0:KD], new_ref[:, KD:2 * KD], bs_ref[0, :, 0:PAGE], mask)

    @pl.when(p > 0)
    def _():
        n = p - 1
        update(False, _gather_heads(pg0_ref, pg1_ref, 2 * NSA_KV_HEADS, NSA_KV_HEADS),
               _gather_heads(pg0_ref, pg1_ref, 3 * NSA_KV_HEADS, NSA_KV_HEADS),
               bs_ref[jnp.minimum(n_steps - n, 2)], picked(n * (2 * PAGE // SEL_BLOCK), 2 * PAGE))

    @pl.when(p == n_steps)
    def _():
        gs = jax.nn.sigmoid(g_ref[...])
        o = ocw_ref[...] + gs[:, 1:2] * (acc_ref[...] / jnp.maximum(l_ref[...], 1e-30))
        for h in range(N_HEADS):
            k = h // NSA_GROUP
            o_ref[:, h * HEAD_DIM:(h + 1) * HEAD_DIM] = o[h * s_len:(h + 1) * s_len, k * HEAD_DIM:(k + 1) * HEAD_DIM]


def _nsa_sample_attn(z, gates, cmp, bias_c, bias_s, bias_w, wall, new_rows, cache, layer, page_table, s_len, past_len):
    nb_, n_pages = page_table.shape
    rows = N_HEADS * s_len
    n_cmp = cmp.shape[2]
    n_sel = 384
    assert past_len // SEL_BLOCK + 1 <= n_sel
    n_w = wall.shape[1]
    n_steps = n_pages // 2
    slots = 4 * NSA_KV_HEADS
    cache5 = cache.reshape(cache.shape[0], cache.shape[1], PAGE * slots, HEAD_DIM)
    pg_spec = lambda e: pl.BlockSpec(
        (None, None, PAGE * slots, HEAD_DIM), lambda b, p, pt: (layer, pt[b, 2 * jnp.maximum(p - 1, 0) + e], 0, 0))
    full = lambda *shape: pl.BlockSpec(shape, lambda b, p, pt: (0,) * len(shape))
    return pl.pallas_call(
        functools.partial(_nsa_sample_kernel, n_steps=n_steps, s_len=s_len, past_len=past_len),
        out_shape=jax.ShapeDtypeStruct((nb_ * s_len, HD), F32),
        grid_spec=pltpu.PrefetchScalarGridSpec(
            num_scalar_prefetch=1, grid=(nb_, n_steps + 1),
            in_specs=[
                pl.BlockSpec((s_len, HD), lambda b, p, pt: (b, 0)),
                pl.BlockSpec((None, rows, 3), lambda b, p, pt: (b, 0, 0)),
                pl.BlockSpec((None, 2, n_cmp, KD), lambda b, p, pt: (b, 0, 0, 0)),
                full(rows, n_cmp), full(3, rows, 2 * PAGE), full(rows, n_w),
                pl.BlockSpec((None, n_w, 2 * KD), lambda b, p, pt: (b, 0, 0)),
                pl.BlockSpec((None, PAGE, 2 * KD), lambda b, p, pt: (b, 0, 0)),
                pg_spec(0), pg_spec(1),
            ],
            out_specs=pl.BlockSpec((s_len, HD), lambda b, p, pt: (b, 0)),
            scratch_shapes=[
                pltpu.VMEM((rows, KD), BF16), pltpu.VMEM((rows, n_sel), F32),
                pltpu.VMEM((rows, 1), F32), pltpu.VMEM((rows, 1), F32),
                pltpu.VMEM((rows, KD), F32), pltpu.VMEM((rows, KD), F32)]),
        compiler_params=_params(("parallel", "arbitrary")),
        name="nsa_sample_attn",
    )(page_table, z, gates, cmp, bias_c, bias_s, bias_w, wall, new_rows, cache5, cache5)


def _pad_cols(w, n):
    return jnp.pad(w, ((0, 0), (0, n - w.shape[1])))


def _tile_gain(parts):
    flags, gains = [], []
    for g, nh in parts:
        flags.append(jnp.full((nh * HEAD_DIM,), 0.0 if g is None else 1.0, F32))
        gains.append(jnp.ones((nh * HEAD_DIM,), F32) if g is None else jnp.tile(g.astype(F32), nh))
    return jnp.concatenate(flags)[None, :], jnp.concatenate(gains)[None, :]


def _half_ffn(xp, xs, g, wg, wu, wd):
    wg = _pad_cols(wg.astype(BF16), D_FF_PAD)
    wu = _pad_cols(wu.astype(BF16), D_FF_PAD)
    wd = jnp.pad(wd.astype(BF16), ((0, D_FF_PAD - D_FF), (0, 0)))
    return _ffn(xp, g, wg, wu, wd, tm=512), _ffn(xs, g, wg, wu, wd, tm=xs.shape[0])


def _moba_layer(xp, xs, nbatch, t, sbatch, s_len, g_mix, w_qkv, w_o, g_q, g_k, cache, layer, page_table, bias_p, bias_s):
    flag, gain = _tile_gain([(g_q, N_HEADS), (g_k, N_HEADS), (None, N_HEADS)])
    w = w_qkv.astype(BF16)
    wo = w_o.astype(BF16)
    yp = _proj(xp, g_mix, w, flag, gain, tm=512, tn=512)
    ys = _proj(xs, g_mix, w, flag, gain, tm=xs.shape[0], tn=512)

    ident = jnp.arange(nbatch * t // PAGE, dtype=I32).reshape(nbatch, t // PAGE)
    km_p = _kmean(yp.reshape(1, nbatch * t // PAGE, PAGE, 3 * HD), 0, 1, ident)
    op = _moba_prompt_attn(yp, km_p, bias_p, nbatch, t)
    xp = _oproj(op, wo, xp, tm=512)

    km_s = _kmean_cache(cache, layer, page_table)
    kv_new = jnp.pad(ys[:, HD:].reshape(sbatch, s_len, 2 * HD), ((0, 0), (0, PAGE - s_len), (0, 0)))
    osm = _moba_sample_attn(ys, km_s, bias_s, kv_new, cache, layer, page_table, s_len)
    xs = _oproj(osm, wo, xs, tm=xs.shape[0])

    kv_p = yp[:, HD:].reshape(nbatch, t, 2, N_HEADS, HEAD_DIM)
    kv_s = ys[:, HD:].reshape(sbatch, s_len, 2, N_HEADS, HEAD_DIM)
    return xp, xs, kv_p, kv_s


def _nsa_layer(xp, xs, nbatch, t, sbatch, s_len, past_len, g_mix, w_in, w_o, g_q, g_k, cmp_pos, cmp_w1, cmp_b1, cmp_w2,
               cache, win_buf, layer, page_table, bias_p, bias_cp, bias_s, bias_c, bias_w):
    flag, gain = _tile_gain([(g_q, N_HEADS), (None, 2 * NSA_KV_HEADS), (g_k[1], NSA_KV_HEADS), (None, NSA_KV_HEADS),
                             (g_k[2], NSA_KV_HEADS), (None, NSA_KV_HEADS)])
    extra = NSA_IN_PAD - flag.shape[1]
    flag = jnp.pad(flag, ((0, 0), (0, extra)))
    gain = jnp.pad(gain, ((0, 0), (0, extra)), constant_values=1.0)
    w = _pad_cols(w_in.astype(BF16), NSA_IN_PAD)
    wo = w_o.astype(BF16)
    zp = _proj(xp, g_mix, w, flag, gain, tm=512, tn=256)
    zs = _proj(xs, g_mix, w, flag, gain, tm=xs.shape[0], tn=256)
    g0 = HD + 6 * KD

    half = CMP_STRIDE * HEAD_DIM
    wcat = jnp.concatenate([cmp_w1[:, :half], cmp_w1[:, half:]], axis=2).astype(BF16)
    pos_flat = jnp.pad(cmp_pos.reshape(2, 1, CMP_LEN * HEAD_DIM), ((0, 0), (0, 7), (0, 0)))
    c0 = _cmp_c0(pos_flat, cmp_w1, cmp_b1.reshape(2, 1, CMP_HIDDEN))
    w2 = cmp_w2.astype(BF16)
    cflag = jnp.stack([jnp.ones((1, HEAD_DIM), F32), jnp.zeros((1, HEAD_DIM), F32)])
    cgain = jnp.stack([g_k[0].astype(F32)[None, :], jnp.ones((1, HEAD_DIM), F32)])

    ident = jnp.arange(nbatch * t // PAGE, dtype=I32).reshape(nbatch, t // PAGE)
    pages_p = zp[:, HD:HD + 4 * KD].reshape(1, nbatch * t // PAGE, PAGE * 4 * NSA_KV_HEADS, HEAD_DIM)
    a_p = _cmp_a(pages_p, 0, ident, wcat)
    cmp_p = _cmp_b(a_p, c0, w2, cflag, cgain)
    gates_p = zp[:, g0:g0 + 3 * N_HEADS].reshape(nbatch * t, NSA_KV_HEADS, 3 * NSA_GROUP).transpose(1, 0, 2)
    op = _nsa_prompt_attn(zp, gates_p, cmp_p, bias_p, bias_cp, nbatch, t)
    xp = _oproj(op, wo, xp, tm=512)

    pages_s = cache.reshape(cache.shape[0], cache.shape[1], PAGE * 4 * NSA_KV_HEADS, HEAD_DIM)
    a_s = _cmp_a(pages_s, layer, page_table, wcat)
    cmp_s = _cmp_b(a_s, c0, w2, cflag, cgain)
    gates_s = zs[:, g0:g0 + 3 * N_HEADS].reshape(sbatch, s_len, N_HEADS, 3).transpose(0, 2, 1, 3)
    gates_s = gates_s.reshape(sbatch, N_HEADS * s_len, 3)
    win_new = zs[:, HD + 4 * KD:g0].reshape(sbatch, s_len, 2 * KD)
    w_all = jnp.concatenate([win_buf.reshape(sbatch, -1, 2 * KD), win_new], axis=1)
    n_w = w_all.shape[1]
    wall = jnp.pad(w_all, ((0, 0), (0, bias_w.shape[1] - n_w), (0, 0)))
    new_rows = jnp.pad(zs[:, HD + 2 * KD:HD + 4 * KD].reshape(sbatch, s_len, 2 * KD), ((0, 0), (0, PAGE - s_len), (0, 0)))
    osm = _nsa_sample_attn(zs, gates_s, cmp_s, bias_c, bias_s, bias_w, wall, new_rows, cache, layer, page_table,
                           s_len, past_len)
    xs = _oproj(osm, wo, xs, tm=xs.shape[0])

    rows_p = zp[:, HD:HD + 4 * KD].reshape(nbatch, t, 4, NSA_KV_HEADS, HEAD_DIM)
    rows_s = zs[:, HD:HD + 4 * KD].reshape(sbatch, s_len, 4, NSA_KV_HEADS, HEAD_DIM)
    keep = min(WINDOW, t)
    win_p = zp[:, HD + 4 * KD:g0].reshape(nbatch, t, 2, NSA_KV_HEADS, HEAD_DIM)[:, t - keep:]
    win_s = w_all[:, s_len:].reshape(sbatch, n_w - s_len, 2, NSA_KV_HEADS, HEAD_DIM)
    return xp, xs, rows_p, rows_s, win_p, win_s


def kernel(x_prompt, x_sample, cache_moba_kv, cache_nsa_kv, state_nsa_win, page_table, rel_bias, ffn_norm, ffn_w_gate, ffn_w_up, ffn_w_down, mix_norm, moba_w_qkv, moba_w_o, moba_q_norm, moba_k_norm, nsa_w_in, nsa_w_o, nsa_q_norm, nsa_k_norm, nsa_cmp_pos, nsa_cmp_w1, nsa_cmp_b1, nsa_cmp_w2):
    nbatch, t, _ = x_prompt.shape
    sbatch, s_len, _ = x_sample.shape
    past_len = page_table.shape[1] * PAGE
    wb = state_nsa_win.shape[2]
    assert t % TQ == 0 and past_len % TQ == 0 and wb == WINDOW and s_len <= 8
    xp = x_prompt.reshape(nbatch * t, D_MODEL)
    xs = x_sample.reshape(sbatch * s_len, D_MODEL)
    rows = N_HEADS * s_len

    bias_p = _bias_tiles(rel_bias, 3, TQ, TQ, 0, TQ)
    bias_s = _bias_tiles(rel_bias, 3, s_len, TQ, 0, TQ).reshape(3, rows, TQ)
    bias_cp = _bias_tiles(rel_bias, t // TQ, TQ, t // CMP_STRIDE, -(CMP_LEN - 1), TQ, CMP_STRIDE)
    n_cmp_s = past_len // CMP_STRIDE
    bias_c = _bias_tiles(rel_bias, 1, s_len, n_cmp_s, past_len - (CMP_LEN - 1), 0, CMP_STRIDE).reshape(rows, n_cmp_s)
    n_w = -(-(wb + s_len) // PAGE) * PAGE
    bias_w = _bias_tiles(rel_bias, 1, s_len, n_w, wb, 0).reshape(rows, n_w)

    moba_p, moba_s, nsa_p, nsa_s, win_p, win_s = [], [], [], [], [], []
    for i in range(DEPTH):
        j = i // 2
        xp, xs = _half_ffn(xp, xs, ffn_norm[i, 0], ffn_w_gate[i, 0], ffn_w_up[i, 0], ffn_w_down[i, 0])
        if i % 2 == 0:
            xp, xs, kv_p, kv_s = _moba_layer(xp, xs, nbatch, t, sbatch, s_len, mix_norm[i], moba_w_qkv[j], moba_w_o[j],
                                             moba_q_norm[j], moba_k_norm[j], cache_moba_kv, j, page_table, bias_p, bias_s)
            moba_p.append(kv_p)
            moba_s.append(kv_s)
        else:
            xp, xs, rp, rs, wp, ws = _nsa_layer(xp, xs, nbatch, t, sbatch, s_len, past_len, mix_norm[i], nsa_w_in[j],
                                                nsa_w_o[j], nsa_q_norm[j], nsa_k_norm[j], nsa_cmp_pos[j], nsa_cmp_w1[j],
                                                nsa_cmp_b1[j], nsa_cmp_w2[j], cache_nsa_kv, state_nsa_win[j], j,
                                                page_table, bias_p, bias_cp, bias_s, bias_c, bias_w)
            nsa_p.append(rp)
            nsa_s.append(rs)
            win_p.append(wp)
            win_s.append(ws)
        xp, xs = _half_ffn(xp, xs, ffn_norm[i, 1], ffn_w_gate[i, 1], ffn_w_up[i, 1], ffn_w_down[i, 1])
    return (xp.reshape(nbatch, t, D_MODEL), xs.reshape(sbatch, s_len, D_MODEL), jnp.stack(moba_p), jnp.stack(moba_s),
            jnp.stack(nsa_p), jnp.stack(nsa_s), jnp.stack(win_p), jnp.stack(win_s))
```

```python
import functools
import math

import jax
import jax.numpy as jnp
from jax import lax
from jax.experimental import pallas as pl
from jax.experimental.pallas import tpu as pltpu

F32 = jnp.float32
BF16 = jnp.bfloat16
I32 = jnp.int32
HIGHEST = lax.Precision.HIGHEST

D_MODEL = 2048
DEPTH = 4
HEAD_DIM = 128
N_HEADS = 16
HD = N_HEADS * HEAD_DIM
D_FF = 5504
D_FF_PAD = 5632
MOBA_BLOCK = 256
MOBA_TOPK = 3
NSA_KV_HEADS = 4
NSA_GROUP = 4
KD = NSA_KV_HEADS * HEAD_DIM
CMP_LEN = 32
CMP_STRIDE = 16
CMP_HIDDEN = 256
SEL_BLOCK = 64
SEL_TOPN = 16
WINDOW = 512
FORCE_BONUS = 1e4
N_BUCKETS = 32
MAX_DISTANCE = 128
RMS_EPS = 1e-6
ATTN_SCALE = HEAD_DIM ** -0.5
PAGE = 128
TQ = 256
PROJ_TM = 1024
NSA_IN = HD + 6 * KD + 3 * N_HEADS
NSA_IN_PAD = 5376
NEG = -1e30
VMEM_LIMIT = 56 * 1024 * 1024

NT = (((1,), (1,)), ((), ()))


def _params(sem, vmem=VMEM_LIMIT):
    return pltpu.CompilerParams(dimension_semantics=sem, vmem_limit_bytes=vmem)


def _dot(a, b, **kw):
    return jnp.dot(a, b, preferred_element_type=F32, **kw)


def _dot_nt(a, b, **kw):
    return lax.dot_general(a, b, NT, preferred_element_type=F32, **kw)


def _iota(shape, dim):
    return lax.broadcasted_iota(I32, shape, dim)


def _rms_scale(x):
    return lax.rsqrt(jnp.mean(x * x, axis=-1, keepdims=True) + RMS_EPS)


def _t5_bucket(dist):
    n = jnp.maximum(dist, 0)
    exact = N_BUCKETS // 2
    nf = jnp.maximum(n, 1).astype(F32)
    log_b = exact + (jnp.log(nf * (1.0 / exact)) / math.log(MAX_DISTANCE / exact) * (N_BUCKETS - exact)).astype(I32)
    return jnp.where(n < exact, n, jnp.minimum(log_b, N_BUCKETS - 1))


def _bias_lookup(bucket, tab_ref, h):
    acc = jnp.zeros(bucket.shape, F32)
    for b in range(N_BUCKETS):
        acc = jnp.where(bucket == b, tab_ref[b, h], acc)
    return acc


def _topk_mask(score, idx, k):
    rank = jnp.zeros(score.shape, I32)
    for m in range(score.shape[1]):
        c = score[:, m:m + 1]
        beats = (c > score) | ((c == score) & (idx > m))
        rank = rank + beats.astype(I32)
    return rank < k


def _pick_col(mat, idx, j):
    return jnp.sum(jnp.where(idx == j, mat, 0.0), axis=1, keepdims=True)


def _softmax_first(s, mask):
    s = jnp.where(mask, s, NEG)
    m = jnp.max(s, axis=1, keepdims=True)
    p = jnp.where(mask, jnp.exp(s - m), 0.0)
    return m, jnp.sum(p, axis=1, keepdims=True), p


def _softmax_next(s, mask, m, l):
    s = jnp.where(mask, s, NEG)
    m_new = jnp.maximum(m, jnp.max(s, axis=1, keepdims=True))
    a = jnp.exp(m - m_new)
    p = jnp.where(mask, jnp.exp(s - m_new), 0.0)
    return m_new, a * l + jnp.sum(p, axis=1, keepdims=True), p, a


def _ffn_kernel(x_ref, g_ref, wg_ref, wu_ref, wd_ref, o_ref, h_ref, acc_ref):
    j = pl.program_id(1)

    @pl.when(j == 0)
    def _():
        x = x_ref[...]
        h_ref[...] = (x * _rms_scale(x) * g_ref[...]).astype(BF16)
        acc_ref[...] = jnp.zeros_like(acc_ref)

    h = h_ref[...]
    a = _dot(h, wg_ref[...])
    b = _dot(h, wu_ref[...])
    t = (a * jax.nn.sigmoid(a) * b).astype(BF16)
    acc_ref[...] += _dot(t, wd_ref[...])

    @pl.when(j == pl.num_programs(1) - 1)
    def _():
        o_ref[...] = x_ref[...] + 0.5 * acc_ref[...]


def _ffn(x, g, wg, wu, wd, tm, tf=512):
    n = x.shape[0]
    fp = wg.shape[1]
    return pl.pallas_call(
        _ffn_kernel,
        out_shape=jax.ShapeDtypeStruct((n, D_MODEL), F32),
        grid=(n // tm, fp // tf),
        in_specs=[
            pl.BlockSpec((tm, D_MODEL), lambda i, j: (i, 0)),
            pl.BlockSpec((1, D_MODEL), lambda i, j: (0, 0)),
            pl.BlockSpec((D_MODEL, tf), lambda i, j: (0, j)),
            pl.BlockSpec((D_MODEL, tf), lambda i, j: (0, j)),
            pl.BlockSpec((tf, D_MODEL), lambda i, j: (j, 0)),
        ],
        out_specs=pl.BlockSpec((tm, D_MODEL), lambda i, j: (i, 0)),
        scratch_shapes=[pltpu.VMEM((tm, D_MODEL), BF16), pltpu.VMEM((tm, D_MODEL), F32)],
        compiler_params=_params(("parallel", "arbitrary")),
        name="ffn",
    )(x, g.reshape(1, D_MODEL), wg, wu, wd)


def _proj_kernel(x_ref, g_ref, w_ref, f_ref, gn_ref, o_ref, h_ref):
    @pl.when(pl.program_id(1) == 0)
    def _():
        x = x_ref[...]
        h_ref[...] = (x * _rms_scale(x) * g_ref[...]).astype(BF16)

    y = _dot(h_ref[...], w_ref[...])
    f = f_ref[...]
    gn = gn_ref[...]
    for c in range(y.shape[1] // HEAD_DIM):
        sl = slice(c * HEAD_DIM, (c + 1) * HEAD_DIM)
        yc = y[:, sl]
        sc = f[:, sl] * _rms_scale(yc) + (1.0 - f[:, sl])
        o_ref[:, sl] = yc * sc * gn[:, sl]


def _proj(x, g, w, flag, gain, tm, tn):
    n = x.shape[0]
    npad = w.shape[1]
    return pl.pallas_call(
        _proj_kernel,
        out_shape=jax.ShapeDtypeStruct((n, npad), F32),
        grid=(n // tm, npad // tn),
        in_specs=[
            pl.BlockSpec((tm, D_MODEL), lambda i, j: (i, 0)),
            pl.BlockSpec((1, D_MODEL), lambda i, j: (0, 0)),
            pl.BlockSpec((D_MODEL, tn), lambda i, j: (0, j)),
            pl.BlockSpec((1, tn), lambda i, j: (0, j)),
            pl.BlockSpec((1, tn), lambda i, j: (0, j)),
        ],
        out_specs=pl.BlockSpec((tm, tn), lambda i, j: (i, j)),
        scratch_shapes=[pltpu.VMEM((tm, D_MODEL), BF16)],
        compiler_params=_params(("parallel", "arbitrary")),
        name="proj",
    )(x, g.reshape(1, D_MODEL), w, flag, gain)


def _oproj_kernel(a_ref, w_ref, x_ref, o_ref):
    o_ref[...] = x_ref[...] + _dot(a_ref[...].astype(BF16), w_ref[...])


def _oproj(a, w, x, tm, tn=512):
    n = a.shape[0]
    return pl.pallas_call(
        _oproj_kernel,
        out_shape=jax.ShapeDtypeStruct((n, D_MODEL), F32),
        grid=(n // tm, D_MODEL // tn),
        in_specs=[
            pl.BlockSpec((tm, HD), lambda i, j: (i, 0)),
            pl.BlockSpec((HD, tn), lambda i, j: (0, j)),
            pl.BlockSpec((tm, tn), lambda i, j: (i, j)),
        ],
        out_specs=pl.BlockSpec((tm, tn), lambda i, j: (i, j)),
        compiler_params=_params(("parallel", "arbitrary")),
        name="oproj",
    )(a, w, x)


def _bias_kernel(tab_ref, o_ref, *, base, stride, cmul):
    h = pl.program_id(0)
    nk, rows, cols = o_ref.shape
    d0 = _iota((rows, cols), 0) - cmul * _iota((rows, cols), 1) + base
    for kind in range(nk):
        d = d0 + kind * stride
        o_ref[kind] = jnp.where(d < 0, NEG, _bias_lookup(_t5_bucket(d), tab_ref, h))


def _bias_tiles(rel_bias, nk, rows, cols, base, stride, cmul=1):
    return pl.pallas_call(
        functools.partial(_bias_kernel, base=base, stride=stride, cmul=cmul),
        out_shape=jax.ShapeDtypeStruct((nk, N_HEADS, rows, cols), F32),
        grid=(N_HEADS,),
        in_specs=[pl.BlockSpec(memory_space=pltpu.SMEM)],
        out_specs=pl.BlockSpec((nk, None, rows, cols), lambda h: (0, h, 0, 0)),
        compiler_params=_params(("arbitrary",)),
        name="bias_tiles",
    )(rel_bias)


def _kmean_kernel(pt_ref, k0_ref, k1_ref, o_ref):
    s = jnp.sum(k0_ref[...], axis=0, keepdims=True) + jnp.sum(k1_ref[...], axis=0, keepdims=True)
    o_ref[...] = s * (1.0 / MOBA_BLOCK)


def _kmean(pages, layer, kcol, page_table):
    nb_, npg = page_table.shape
    nblk = npg // 2
    spec = lambda e: pl.BlockSpec((None, None, PAGE, HD), lambda b, n, pt: (layer, pt[b, 2 * n + e], 0, kcol))
    out = pl.pallas_call(
        _kmean_kernel,
        out_shape=jax.ShapeDtypeStruct((nb_, nblk, 1, HD), F32),
        grid_spec=pltpu.PrefetchScalarGridSpec(
            num_scalar_prefetch=1, grid=(nb_, nblk),
            in_specs=[spec(0), spec(1)],
            out_specs=pl.BlockSpec((None, None, 1, HD), lambda b, n, pt: (b, n, 0, 0))),
        compiler_params=_params(("parallel", "arbitrary")),
        name="kmean",
    )(page_table, pages, pages)
    return out.reshape(nb_, nblk, HD)


KMEAN_BLOCKS = 2


def _kmean_cache_kernel(pt_ref, *refs):
    o_ref = refs[-1]
    for e in range(KMEAN_BLOCKS):
        o_ref[e] = (jnp.sum(refs[2 * e][...], axis=0) + jnp.sum(refs[2 * e + 1][...], axis=0)) * (1.0 / MOBA_BLOCK)


def _kmean_cache(cache, layer, page_table):
    nb_, npg = page_table.shape
    nblk = npg // 2
    per = 2 * KMEAN_BLOCKS
    assert nblk % KMEAN_BLOCKS == 0
    rows = cache.reshape(cache.shape[0], cache.shape[1], PAGE, 2 * N_HEADS, HEAD_DIM)
    spec = lambda e: pl.BlockSpec((None, None, PAGE, N_HEADS, HEAD_DIM),
                                  lambda b, n, pt: (layer, pt[b, per * n + e], 0, 0, 0))
    out = pl.pallas_call(
        _kmean_cache_kernel,
        out_shape=jax.ShapeDtypeStruct((nb_, nblk, N_HEADS, HEAD_DIM), F32),
        grid_spec=pltpu.PrefetchScalarGridSpec(
            num_scalar_prefetch=1, grid=(nb_, nblk // KMEAN_BLOCKS),
            in_specs=[spec(e) for e in range(per)],
            out_specs=pl.BlockSpec((None, KMEAN_BLOCKS, N_HEADS, HEAD_DIM), lambda b, n, pt: (b, n, 0, 0))),
        compiler_params=_params(("parallel", "arbitrary")),
        name="kmean_cache",
    )(page_table, *([rows] * per))
    return out.reshape(nb_, nblk, HD)


MOBA_HPS = 4


def _moba_prompt_kernel(q_ref, k_ref, v_ref, km_ref, bias_ref, o_ref):
    i = pl.program_id(2)
    nb = km_ref.shape[0]
    jb = _iota((nb, TQ), 0)
    past = jb < i
    lane = _iota((TQ, HEAD_DIM), 1)
    heads = [slice(e * HEAD_DIM, (e + 1) * HEAD_DIM) for e in range(MOBA_HPS)]

    q_aug = []
    for sl in heads:
        q = q_ref[:, sl]
        gate = jnp.where(past, _dot_nt(km_ref[:, sl], q, precision=HIGHEST), -jnp.inf)
        rank = jnp.zeros((nb, TQ), I32)
        for m in range(nb):
            c = gate[m:m + 1, :]
            rank = rank + ((c > gate) | ((c == gate) & (jb > m))).astype(I32)
        pen = jnp.where((rank < MOBA_TOPK) & past, 0.0, NEG)
        pen = jnp.concatenate([pen, jnp.zeros((HEAD_DIM - nb, TQ), F32)], axis=0).T
        q_aug.append(jnp.concatenate([(q * ATTN_SCALE).astype(BF16), pen.astype(BF16)], axis=1))

    def scores(j, hot, kind):
        rows = pl.ds(pl.multiple_of(j * TQ, TQ), TQ)
        parts = [_dot_nt(q_aug[e], jnp.concatenate([k_ref[rows, heads[e]].astype(BF16), hot], axis=1))
                 for e in range(MOBA_HPS)]
        return jnp.concatenate(parts, axis=0) + bias_ref[kind].reshape(MOBA_HPS * TQ, TQ)

    ones = jnp.ones((TQ, HEAD_DIM), BF16)

    def weighted(p, j):
        rows = pl.ds(pl.multiple_of(j * TQ, TQ), TQ)
        return jnp.concatenate(
            [_dot(p[e * TQ:(e + 1) * TQ], jnp.concatenate([v_ref[rows, heads[e]].astype(BF16), ones], axis=1))
             for e in range(MOBA_HPS)], axis=0)

    s = scores(i, jnp.zeros((TQ, HEAD_DIM), BF16), 0)
    m = jnp.max(s, axis=1, keepdims=True)
    carry = (m, weighted(jnp.exp(s - m).astype(BF16), i))

    def body(j, carry):
        m, acc = carry
        s = scores(j, jnp.where(lane == j, 1.0, 0.0).astype(BF16), jnp.minimum(i - j, 2))
        m_new = jnp.maximum(m, jnp.max(s, axis=1, keepdims=True))
        return m_new, jnp.exp(m - m_new) * acc + weighted(jnp.exp(s - m_new).astype(BF16), j)

    _, acc = lax.fori_loop(0, i, body, carry)
    o = acc[:, 0:HEAD_DIM] / acc[:, HEAD_DIM:2 * HEAD_DIM]
    for e in range(MOBA_HPS):
        o_ref[:, heads[e]] = o[e * TQ:(e + 1) * TQ]


def _moba_prompt_attn(y, kmean, bias, nbatch, t):
    nq = t // TQ
    assert nq <= HEAD_DIM
    w = MOBA_HPS * HEAD_DIM
    ngrp = N_HEADS // MOBA_HPS
    return pl.pallas_call(
        _moba_prompt_kernel,
        out_shape=jax.ShapeDtypeStruct((nbatch * t, HD), F32),
        grid=(nbatch, ngrp, nq),
        in_specs=[
            pl.BlockSpec((TQ, w), lambda b, h, i: (b * nq + i, h)),
            pl.BlockSpec((t, w), lambda b, h, i: (b, ngrp + h)),
            pl.BlockSpec((t, w), lambda b, h, i: (b, 2 * ngrp + h)),
            pl.BlockSpec((None, nq, w), lambda b, h, i: (b, 0, h)),
            pl.BlockSpec((3, MOBA_HPS, TQ, TQ), lambda b, h, i: (0, h, 0, 0)),
        ],
        out_specs=pl.BlockSpec((TQ, w), lambda b, h, i: (b * nq + i, h)),
        compiler_params=_params(("parallel", "parallel", "arbitrary")),
        name="moba_prompt_attn",
    )(y, y, y, kmean, bias)


def _gather_heads(r0, r1, h0, nh):
    slots = r0.shape[0] // PAGE
    return jnp.concatenate(
        [jnp.concatenate([r[pl.ds(h0 + h, PAGE, stride=slots), :] for r in (r0, r1)], axis=0) for h in range(nh)],
        axis=1)


def _moba_sample_kernel(pt_ref, q_ref, km_ref, bias_ref, knew_ref, vnew_ref, pg0_ref, pg1_ref, o_ref,
                        qb_ref, sel_ref, m_ref, l_ref, acc_ref, stage_ref, *, s_len):
    p = pl.program_id(1)
    nblk = km_ref.shape[0]
    rows = N_HEADS * s_len
    nidx = _iota((rows, nblk), 1)

    def update(first, k, v, bias, mask):
        s = _dot_nt(qb_ref[...], k.astype(BF16)) * ATTN_SCALE + bias
        if first:
            m, l, pr = _softmax_first(s, mask)
            acc_ref[...] = _dot(pr.astype(BF16), v.astype(BF16))
        else:
            m, l, pr, a = _softmax_next(s, mask, m_ref[...], l_ref[...])
            acc_ref[...] = a * acc_ref[...] + _dot(pr.astype(BF16), v.astype(BF16))
        m_ref[...] = m
        l_ref[...] = l

    @pl.when(p == 0)
    def _():
        q = q_ref[...]
        blocks = []
        for h in range(N_HEADS):
            sl = slice(h * HEAD_DIM, (h + 1) * HEAD_DIM)
            lane_ok = (_iota((s_len, HD), 1) // HEAD_DIM) == h
            blocks.append(jnp.where(lane_ok, jnp.concatenate([q[:, sl]] * N_HEADS, axis=1), 0.0))
        qbd = jnp.concatenate(blocks, axis=0)
        gate = _dot_nt(qbd, km_ref[...], precision=HIGHEST)
        sel_ref[...] = _topk_mask(gate, nidx, MOBA_TOPK).astype(F32)
        qb_ref[...] = qbd.astype(BF16)
        t = _iota((rows, PAGE), 0) % s_len
        update(True, knew_ref[...], vnew_ref[...], bias_ref[0, :, 0:PAGE], _iota((rows, PAGE), 1) <= t)

    @pl.when(p > 0)
    def _():
        n = p - 1
        picked = _pick_col(sel_ref[...], nidx, n) > 0.5
        mask = jnp.broadcast_to(picked, (rows, MOBA_BLOCK))
        slots = 2 * N_HEADS
        for e, pg in enumerate((pg0_ref, pg1_ref)):
            for a in range(4):
                stage_ref[e, a] = pg[pl.ds(a, PAGE * slots // 4, stride=4), :]

        def head_rows(slot):
            return jnp.concatenate(
                [stage_ref[e, slot % 4, pl.ds(slot // 4, PAGE, stride=slots // 4), :] for e in range(2)], axis=0)

        k = jnp.concatenate([head_rows(h) for h in range(N_HEADS)], axis=1)
        v = jnp.concatenate([head_rows(N_HEADS + h) for h in range(N_HEADS)], axis=1)
        update(False, k, v, bias_ref[jnp.minimum(nblk - n, 2)], mask)

    @pl.when(p == nblk)
    def _():
        inv = 1.0 / l_ref[...]
        for h in range(N_HEADS):
            sl = slice(h * HEAD_DIM, (h + 1) * HEAD_DIM)
            rs = slice(h * s_len, (h + 1) * s_len)
            o_ref[:, sl] = acc_ref[rs, sl] * inv[rs]


def _moba_sample_attn(y, kmean, bias, kv_new, cache, layer, page_table, s_len):
    nb_, n_pages = page_table.shape
    nblk = n_pages // 2
    rows = N_HEADS * s_len
    slots = 2 * N_HEADS
    cache5 = cache.reshape(cache.shape[0], cache.shape[1], PAGE * slots, HEAD_DIM)
    pg_spec = lambda e: pl.BlockSpec(
        (None, None, PAGE * slots, HEAD_DIM), lambda b, p, pt: (layer, pt[b, 2 * jnp.maximum(p - 1, 0) + e], 0, 0))
    return pl.pallas_call(
        functools.partial(_moba_sample_kernel, s_len=s_len),
        out_shape=jax.ShapeDtypeStruct((nb_ * s_len, HD), F32),
        grid_spec=pltpu.PrefetchScalarGridSpec(
            num_scalar_prefetch=1, grid=(nb_, nblk + 1),
            in_specs=[
                pl.BlockSpec((s_len, HD), lambda b, p, pt: (b, 0)),
                pl.BlockSpec((None, nblk, HD), lambda b, p, pt: (b, 0, 0)),
                pl.BlockSpec((3, rows, MOBA_BLOCK), lambda b, p, pt: (0, 0, 0)),
                pl.BlockSpec((None, PAGE, HD), lambda b, p, pt: (b, 0, 0)),
                pl.BlockSpec((None, PAGE, HD), lambda b, p, pt: (b, 0, 1)),
                pg_spec(0), pg_spec(1),
            ],
            out_specs=pl.BlockSpec((s_len, HD), lambda b, p, pt: (b, 0)),
            scratch_shapes=[
                pltpu.VMEM((rows, HD), BF16), pltpu.VMEM((rows, nblk), F32),
                pltpu.VMEM((rows, 1), F32), pltpu.VMEM((rows, 1), F32), pltpu.VMEM((rows, HD), F32),
                pltpu.VMEM((2, 4, PAGE * slots // 4, HEAD_DIM), F32)]),
        compiler_params=_params(("parallel", "arbitrary")),
        name="moba_sample_attn",
    )(page_table, y, kmean, bias, kv_new, kv_new, cache5, cache5)


def _cmp_a_kernel(pt_ref, *refs, pg):
    x_refs, w_ref, o_ref, lhs_ref = refs[:pg], refs[pg], refs[pg + 1], refs[pg + 2]
    n_str = PAGE // CMP_STRIDE
    slots = 4 * NSA_KV_HEADS
    for kind in range(2):
        for c in range(pg):
            x = x_refs[c]
            for k in range(NSA_KV_HEADS):
                r0 = (k * pg + c) * n_str
                slot = kind * NSA_KV_HEADS + k
                for l in range(CMP_STRIDE):
                    lhs_ref[kind, r0:r0 + n_str, l * HEAD_DIM:(l + 1) * HEAD_DIM] = (
                        x[pl.ds(l * slots + slot, n_str, stride=CMP_STRIDE * slots), :])
        res = _dot(lhs_ref[kind].astype(BF16), w_ref[kind])
        for k in range(NSA_KV_HEADS):
            o_ref[kind, k] = res[k * pg * n_str:(k + 1) * pg * n_str]


def _cmp_a(pages, layer, page_table, wcat, pg=8):
    nb_, npg = page_table.shape
    n_str = PAGE // CMP_STRIDE
    spec = lambda c: pl.BlockSpec((None, None, PAGE * 4 * NSA_KV_HEADS, HEAD_DIM),
                                  lambda b, s, pt: (layer, pt[b, s * pg + c], 0, 0))
    return pl.pallas_call(
        functools.partial(_cmp_a_kernel, pg=pg),
        out_shape=jax.ShapeDtypeStruct((nb_, 2, NSA_KV_HEADS, npg * n_str, 2 * CMP_HIDDEN), F32),
        grid_spec=pltpu.PrefetchScalarGridSpec(
            num_scalar_prefetch=1, grid=(nb_, npg // pg),
            in_specs=[spec(c) for c in range(pg)] + [
                pl.BlockSpec((2, CMP_STRIDE * HEAD_DIM, 2 * CMP_HIDDEN), lambda b, s, pt: (0, 0, 0))],
            out_specs=pl.BlockSpec((None, 2, NSA_KV_HEADS, pg * n_str, 2 * CMP_HIDDEN),
                                   lambda b, s, pt: (b, 0, 0, s, 0)),
            scratch_shapes=[pltpu.VMEM((2, NSA_KV_HEADS * pg * n_str, CMP_STRIDE * HEAD_DIM), F32)]),
        compiler_params=_params(("parallel", "arbitrary")),
        name="cmp_a",
    )(page_table, *([pages] * pg), wcat)


def _c0_kernel(p_ref, w_ref, b_ref, o_ref):
    o_ref[...] = _dot(p_ref[...], w_ref[...]) + b_ref[...]


def _cmp_c0(pos_flat, w1, b1):
    return pl.pallas_call(
        _c0_kernel,
        out_shape=jax.ShapeDtypeStruct((2, 8, CMP_HIDDEN), F32),
        grid=(2,),
        in_specs=[
            pl.BlockSpec((None, 8, CMP_LEN * HEAD_DIM), lambda i: (i, 0, 0)),
            pl.BlockSpec((None, CMP_LEN * HEAD_DIM, CMP_HIDDEN), lambda i: (i, 0, 0)),
            pl.BlockSpec((None, 1, CMP_HIDDEN), lambda i: (i, 0, 0)),
        ],
        out_specs=pl.BlockSpec((None, 8, CMP_HIDDEN), lambda i: (i, 0, 0)),
        compiler_params=_params(("arbitrary",)),
        name="cmp_c0",
    )(pos_flat, w1, b1)


def _cmp_b_kernel(a_ref, c0_ref, w2_ref, f_ref, g_ref, o_ref):
    n = a_ref.shape[0]
    a0 = a_ref[:, 0:CMP_HIDDEN]
    a1 = pltpu.roll(a_ref[:, CMP_HIDDEN:2 * CMP_HIDDEN], n - 1, 0)
    h1 = a0 + a1 + c0_ref[0:1, :]
    y = _dot(jax.nn.gelu(h1).astype(BF16), w2_ref[...])
    f = f_ref[...]
    o_ref[...] = y * (f * _rms_scale(y) + (1.0 - f)) * g_ref[...]


def _cmp_b(a, c0, w2, flag, gain):
    nb_, _, _, n_str, _ = a.shape
    return pl.pallas_call(
        _cmp_b_kernel,
        out_shape=jax.ShapeDtypeStruct((nb_, 2, n_str, KD), F32),
        grid=(nb_, 2, NSA_KV_HEADS),
        in_specs=[
            pl.BlockSpec((None, None, None, n_str, 2 * CMP_HIDDEN), lambda b, c, k: (b, c, k, 0, 0)),
            pl.BlockSpec((None, 8, CMP_HIDDEN), lambda b, c, k: (c, 0, 0)),
            pl.BlockSpec((None, CMP_HIDDEN, HEAD_DIM), lambda b, c, k: (c, 0, 0)),
            pl.BlockSpec((None, 1, HEAD_DIM), lambda b, c, k: (c, 0, 0)),
            pl.BlockSpec((None, 1, HEAD_DIM), lambda b, c, k: (c, 0, 0)),
        ],
        out_specs=pl.BlockSpec((None, None, n_str, HEAD_DIM), lambda b, c, k: (b, c, 0, k)),
        compiler_params=_params(("parallel", "arbitrary", "arbitrary")),
        name="cmp_b",
    )(a, c0, w2, flag, gain)


def _stride_to_block_matrix(n_cmp, n_blk, transpose):
    shape = (n_blk, n_cmp) if transpose else (n_cmp, n_blk)
    n = _iota(shape, 1 if transpose else 0)
    j = _iota(shape, 0 if transpose else 1)
    r = n - (SEL_BLOCK // CMP_STRIDE) * j
    return jnp.where((r == -1) | (r == 3), 1.0, jnp.where((r >= 0) & (r <= 2), 2.0, 0.0)).astype(F32)


def _nsa_prompt_kernel(q_ref, g_ref, kc_ref, vc_ref, ks_ref, vs_ref, kw_ref, vw_ref, bias_ref, bc_ref, o_ref):
    i = pl.program_id(2)
    q0 = i * TQ
    G = NSA_GROUP
    n_cmp = kc_ref.shape[0]
    n_blk = ks_ref.shape[0] // SEL_BLOCK
    qg = jnp.concatenate([q_ref[:, g * HEAD_DIM:(g + 1) * HEAD_DIM] * ATTN_SCALE for g in range(G)], axis=0).astype(BF16)

    tq = q0 + _iota((TQ, n_cmp), 0)
    dist = tq - (CMP_STRIDE * _iota((TQ, n_cmp), 1) + CMP_LEN - 1)
    vis = jnp.concatenate([dist >= 0] * G, axis=0)
    sc = _dot_nt(qg, kc_ref[...].astype(BF16)) + bc_ref[...].reshape(G * TQ, n_cmp)
    _, lc, pc = _softmax_first(sc, vis)
    pc = pc / jnp.maximum(lc, 1e-30)
    oc = _dot(pc.astype(BF16), vc_ref[...].astype(BF16))

    pcs = pc[0:TQ]
    for g in range(1, G):
        pcs = pcs + pc[g * TQ:(g + 1) * TQ]
    score = _dot_nt(_stride_to_block_matrix(n_cmp, n_blk, True), pcs, precision=HIGHEST)
    jb = _iota((n_blk, TQ), 0)
    cur = (q0 + _iota((n_blk, TQ), 1)) // SEL_BLOCK
    forced = (jb == 0) | (jb == cur) | (jb == cur - 1)
    score = jnp.where(jb <= cur, score + jnp.where(forced, FORCE_BONUS, 0.0), -jnp.inf)
    rank = jnp.zeros((n_blk, TQ), I32)
    for m in range(n_blk):
        c = score[m:m + 1, :]
        rank = rank + ((c > score) | ((c == score) & (jb > m))).astype(I32)
    pen = jnp.where((rank < SEL_TOPN) & (jb <= cur), 0.0, NEG)
    pen = jnp.concatenate([pen, jnp.zeros((HEAD_DIM - n_blk, TQ), F32)], axis=0).T.astype(BF16)
    q_sel = jnp.concatenate([qg, jnp.concatenate([pen] * G, axis=0)], axis=1)

    row = _iota((TQ, TQ), 0)
    col = _iota((TQ, TQ), 1)
    lane = _iota((TQ, HEAD_DIM), 1)
    key_blk = _iota((TQ, HEAD_DIM), 0) // SEL_BLOCK
    ones = jnp.ones((TQ, HEAD_DIM), BF16)

    def rows_of(j):
        return pl.ds(pl.multiple_of(j * TQ, TQ), TQ)

    def values(v_ref, j):
        return jnp.concatenate([v_ref[rows_of(j), :].astype(BF16), ones], axis=1)

    def online(s, v_aug, carry):
        if carry is None:
            m = jnp.max(s, axis=1, keepdims=True)
            return m, _dot(jnp.exp(s - m).astype(BF16), v_aug)
        m, acc = carry
        m_new = jnp.maximum(m, jnp.max(s, axis=1, keepdims=True))
        return m_new, jnp.exp(m - m_new) * acc + _dot(jnp.exp(s - m_new).astype(BF16), v_aug)

    def normalised(acc):
        return acc[:, 0:HEAD_DIM] / acc[:, HEAD_DIM:2 * HEAD_DIM]

    def sel_scores(j, kind):
        hot = jnp.where(lane == j * (TQ // SEL_BLOCK) + key_blk, 1.0, 0.0).astype(BF16)
        k_aug = jnp.concatenate([ks_ref[rows_of(j), :].astype(BF16), hot], axis=1)
        return _dot_nt(q_sel, k_aug) + bias_ref[kind].reshape(G * TQ, TQ)

    carry = online(sel_scores(i, 0), values(vs_ref, i), None)
    _, accs = lax.fori_loop(
        0, i, lambda j, c: online(sel_scores(j, jnp.minimum(i - j, 2)), values(vs_ref, j), c), carry)
    osel = normalised(accs)

    def win_scores(j, kind):
        return _dot_nt(qg, kw_ref[rows_of(j), :].astype(BF16)) + bias_ref[kind].reshape(G * TQ, TQ)

    carry = online(win_scores(i, 0), values(vw_ref, i), None)

    def wbody(d, c):
        mask = (col - row) > jnp.where(d == 1, -TQ, 0)
        s = jnp.where(jnp.concatenate([mask] * G, axis=0), win_scores(i - d, d), NEG)
        return online(s, values(vw_ref, i - d), c)

    _, accw = lax.fori_loop(1, jnp.minimum(i, WINDOW // TQ) + 1, wbody, carry)
    ow = normalised(accw)

    gs = jax.nn.sigmoid(g_ref[...])
    for g in range(G):
        rs = slice(g * TQ, (g + 1) * TQ)
        o_ref[:, g * HEAD_DIM:(g + 1) * HEAD_DIM] = (
            gs[:, 3 * g:3 * g + 1] * oc[rs] + gs[:, 3 * g + 1:3 * g + 2] * osel[rs] + gs[:, 3 * g + 2:3 * g + 3] * ow[rs])


def _nsa_prompt_attn(z, gates, cmp, bias, bias_c, nbatch, t):
    nq = t // TQ
    n_cmp = t // CMP_STRIDE
    kv0 = HD // HEAD_DIM
    kv_spec = lambda kind: pl.BlockSpec((t, HEAD_DIM), lambda b, k, i: (b, kv0 + kind * NSA_KV_HEADS + k))
    return pl.pallas_call(
        _nsa_prompt_kernel,
        out_shape=jax.ShapeDtypeStruct((nbatch * t, HD), F32),
        grid=(nbatch, NSA_KV_HEADS, nq),
        in_specs=[
            pl.BlockSpec((TQ, NSA_GROUP * HEAD_DIM), lambda b, k, i: (b * nq + i, k)),
            pl.BlockSpec((None, TQ, 3 * NSA_GROUP), lambda b, k, i: (k, b * nq + i, 0)),
            pl.BlockSpec((None, None, n_cmp, HEAD_DIM), lambda b, k, i: (b, 0, 0, k)),
            pl.BlockSpec((None, None, n_cmp, HEAD_DIM), lambda b, k, i: (b, 1, 0, k)),
            kv_spec(2), kv_spec(3), kv_spec(4), kv_spec(5),
            pl.BlockSpec((3, NSA_GROUP, TQ, TQ), lambda b, k, i: (0, k, 0, 0)),
            pl.BlockSpec((None, NSA_GROUP, TQ, n_cmp), lambda b, k, i: (i, k, 0, 0)),
        ],
        out_specs=pl.BlockSpec((TQ, NSA_GROUP * HEAD_DIM), lambda b, k, i: (b * nq + i, k)),
        compiler_params=_params(("parallel", "parallel", "arbitrary")),
        name="nsa_prompt_attn",
    )(z, gates, cmp, cmp, z, z, z, z, bias, bias_c)


def _nsa_sample_kernel(pt_ref, q_ref, g_ref, cmp_ref, bc_ref, bs_ref, bw_ref, wall_ref, new_ref, pg0_ref, pg1_ref,
                       o_ref, qb_ref, sel_ref, m_ref, l_ref, acc_ref, ocw_ref, *, n_steps, s_len, past_len):
    p = pl.program_id(1)
    rows = N_HEADS * s_len
    n_cmp = cmp_ref.shape[1]
    n_sel = sel_ref.shape[1]
    jb = _iota((rows, n_sel), 1)

    def update(first, k, v, bias, mask):
        s = _dot_nt(qb_ref[...], k.astype(BF16)) * ATTN_SCALE + bias
        if first:
            m, l, pr = _softmax_first(s, mask)
            acc_ref[...] = _dot(pr.astype(BF16), v.astype(BF16))
        else:
            m, l, pr, a = _softmax_next(s, mask, m_ref[...], l_ref[...])
            acc_ref[...] = a * acc_ref[...] + _dot(pr.astype(BF16), v.astype(BF16))
        m_ref[...] = m
        l_ref[...] = l

    def picked(blk0, n_keys):
        e = (_iota((n_sel, n_keys), 0) == blk0 + _iota((n_sel, n_keys), 1) // SEL_BLOCK)
        return _dot(sel_ref[...].astype(BF16), e.astype(BF16)) > 0.5

    @pl.when(p == 0)
    def _():
        q = q_ref[...]
        blocks = []
        for h in range(N_HEADS):
            lane_ok = (_iota((s_len, KD), 1) // HEAD_DIM) == (h // NSA_GROUP)
            qh = q[:, h * HEAD_DIM:(h + 1) * HEAD_DIM]
            blocks.append(jnp.where(lane_ok, jnp.concatenate([qh] * NSA_KV_HEADS, axis=1), 0.0))
        qb = jnp.concatenate(blocks, axis=0).astype(BF16)
        qb_ref[...] = qb
        gs = jax.nn.sigmoid(g_ref[...])

        t_c = _iota((rows, n_cmp), 0) % s_len
        vis = CMP_STRIDE * _iota((rows, n_cmp), 1) + (CMP_LEN - 1) <= past_len + t_c
        sc = _dot_nt(qb, cmp_ref[0].astype(BF16)) * ATTN_SCALE + bc_ref[...]
        _, lc, pc = _softmax_first(sc, vis)
        pc = pc / jnp.maximum(lc, 1e-30)
        oc = _dot(pc.astype(BF16), cmp_ref[1].astype(BF16))

        r = _iota((rows, rows), 0)
        c = _iota((rows, rows), 1)
        same = ((r // (NSA_GROUP * s_len)) == (c // (NSA_GROUP * s_len))) & ((r % s_len) == (c % s_len))
        pcs = _dot(same.astype(F32), pc, precision=HIGHEST)
        score = _dot(pcs, _stride_to_block_matrix(n_cmp, n_sel, False), precision=HIGHEST)
        cur = (past_len + _iota((rows, n_sel), 0) % s_len) // SEL_BLOCK
        forced = (jb == 0) | (jb == cur) | (jb == cur - 1)
        score = jnp.where(jb <= cur, score + jnp.where(forced, FORCE_BONUS, 0.0), -jnp.inf)
        sel_ref[...] = (_topk_mask(score, jb, SEL_TOPN) & (jb <= cur)).astype(F32)

        n_w = wall_ref.shape[0]
        dw = WINDOW + _iota((rows, n_w), 0) % s_len - _iota((rows, n_w), 1)
        sw = _dot_nt(qb, wall_ref[:, 0:KD].astype(BF16)) * ATTN_SCALE + bw_ref[...]
        _, lw, pw = _softmax_first(sw, (dw >= 0) & (dw < WINDOW))
        pw = pw / jnp.maximum(lw, 1e-30)
        ow = _dot(pw.astype(BF16), wall_ref[:, KD:2 * KD].astype(BF16))
        ocw_ref[...] = gs[:, 0:1] * oc + gs[:, 2:3] * ow

        t_s = _iota((rows, PAGE), 0) % s_len
        mask = picked(past_len // SEL_BLOCK, PAGE) & (_iota((rows, PAGE), 1) <= t_s)
        update(True, new_ref[:, 0:KD], new_ref[:, KD:2 * KD], bs_ref[0, :, 0:PAGE], mask)

    @pl.when(p > 0)
    def _():
        n = p - 1
        update(False, _gather_heads(pg0_ref, pg1_ref, 2 * NSA_KV_HEADS, NSA_KV_HEADS),
               _gather_heads(pg0_ref, pg1_ref, 3 * NSA_KV_HEADS, NSA_KV_HEADS),
               bs_ref[jnp.minimum(n_steps - n, 2)], picked(n * (2 * PAGE // SEL_BLOCK), 2 * PAGE))

    @pl.when(p == n_steps)
    def _():
        gs = jax.nn.sigmoid(g_ref[...])
        o = ocw_ref[...] + gs[:, 1:2] * (acc_ref[...] / jnp.maximum(l_ref[...], 1e-30))
        for h in range(N_HEADS):
            k = h // NSA_GROUP
            o_ref[:, h * HEAD_DIM:(h + 1) * HEAD_DIM] = o[h * s_len:(h + 1) * s_len, k * HEAD_DIM:(k + 1) * HEAD_DIM]


def _nsa_sample_attn(z, gates, cmp, bias_c, bias_s, bias_w, wall, new_rows, cache, layer, page_table, s_len, past_len):
    nb_, n_pages = page_table.shape
    rows = N_HEADS * s_len
    n_cmp = cmp.shape[2]
    n_sel = 384
    assert past_len // SEL_BLOCK + 1 <= n_sel
    n_w = wall.shape[1]
    n_steps = n_pages // 2
    slots = 4 * NSA_KV_HEADS
    cache5 = cache.reshape(cache.shape[0], cache.shape[1], PAGE * slots, HEAD_DIM)
    pg_spec = lambda e: pl.BlockSpec(
        (None, None, PAGE * slots, HEAD_DIM), lambda b, p, pt: (layer, pt[b, 2 * jnp.maximum(p - 1, 0) + e], 0, 0))
    full = lambda *shape: pl.BlockSpec(shape, lambda b, p, pt: (0,) * len(shape))
    return pl.pallas_call(
        functools.partial(_nsa_sample_kernel, n_steps=n_steps, s_len=s_len, past_len=past_len),
        out_shape=jax.ShapeDtypeStruct((nb_ * s_len, HD), F32),
        grid_spec=pltpu.PrefetchScalarGridSpec(
            num_scalar_prefetch=1, grid=(nb_, n_steps + 1),
            in_specs=[
                pl.BlockSpec((s_len, HD), lambda b, p, pt: (b, 0)),
                pl.BlockSpec((None, rows, 3), lambda b, p, pt: (b, 0, 0)),
                pl.BlockSpec((None, 2, n_cmp, KD), lambda b, p, pt: (b, 0, 0, 0)),
                full(rows, n_cmp), full(3, rows, 2 * PAGE), full(rows, n_w),
                pl.BlockSpec((None, n_w, 2 * KD), lambda b, p, pt: (b, 0, 0)),
                pl.BlockSpec((None, PAGE, 2 * KD), lambda b, p, pt: (b, 0, 0)),
                pg_spec(0), pg_spec(1),
            ],
            out_specs=pl.BlockSpec((s_len, HD), lambda b, p, pt: (b, 0)),
            scratch_shapes=[
                pltpu.VMEM((rows, KD), BF16), pltpu.VMEM((rows, n_sel), F32),
                pltpu.VMEM((rows, 1), F32), pltpu.VMEM((rows, 1), F32),
                pltpu.VMEM((rows, KD), F32), pltpu.VMEM((rows, KD), F32)]),
        compiler_params=_params(("parallel", "arbitrary")),
        name="nsa_sample_attn",
    )(page_table, z, gates, cmp, bias_c, bias_s, bias_w, wall, new_rows, cache5, cache5)


def _pad_cols(w, n):
    return jnp.pad(w, ((0, 0), (0, n - w.shape[1])))


def _tile_gain(parts):
    flags, gains = [], []
    for g, nh in parts:
        flags.append(jnp.full((nh * HEAD_DIM,), 0.0 if g is None else 1.0, F32))
        gains.append(jnp.ones((nh * HEAD_DIM,), F32) if g is None else jnp.tile(g.astype(F32), nh))
    return jnp.concatenate(flags)[None, :], jnp.concatenate(gains)[None, :]


def _half_ffn(xp, xs, g, wg, wu, wd):
    wg = _pad_cols(wg.astype(BF16), D_FF_PAD)
    wu = _pad_cols(wu.astype(BF16), D_FF_PAD)
    wd = jnp.pad(wd.astype(BF16), ((0, D_FF_PAD - D_FF), (0, 0)))
    return _ffn(xp, g, wg, wu, wd, tm=512), _ffn(xs, g, wg, wu, wd, tm=xs.shape[0])


def _moba_layer(xp, xs, nbatch, t, sbatch, s_len, g_mix, w_qkv, w_o, g_q, g_k, cache, layer, page_table, bias_p, bias_s):
    flag, gain = _tile_gain([(g_q, N_HEADS), (g_k, N_HEADS), (None, N_HEADS)])
    w = w_qkv.astype(BF16)
    wo = w_o.astype(BF16)
    yp = _proj(xp, g_mix, w, flag, gain, tm=PROJ_TM, tn=512)
    ys = _proj(xs, g_mix, w, flag, gain, tm=xs.shape[0], tn=512)

    ident = jnp.arange(nbatch * t // PAGE, dtype=I32).reshape(nbatch, t // PAGE)
    km_p = _kmean(yp.reshape(1, nbatch * t // PAGE, PAGE, 3 * HD), 0, 1, ident)
    op = _moba_prompt_attn(yp, km_p, bias_p, nbatch, t)
    xp = _oproj(op, wo, xp, tm=PROJ_TM)

    km_s = _kmean_cache(cache, layer, page_table)
    kv_new = jnp.pad(ys[:, HD:].reshape(sbatch, s_len, 2 * HD), ((0, 0), (0, PAGE - s_len), (0, 0)))
    osm = _moba_sample_attn(ys, km_s, bias_s, kv_new, cache, layer, page_table, s_len)
    xs = _oproj(osm, wo, xs, tm=xs.shape[0])

    kv_p = yp[:, HD:].reshape(nbatch, t, 2, N_HEADS, HEAD_DIM)
    kv_s = ys[:, HD:].reshape(sbatch, s_len, 2, N_HEADS, HEAD_DIM)
    return xp, xs, kv_p, kv_s


def _nsa_layer(xp, xs, nbatch, t, sbatch, s_len, past_len, g_mix, w_in, w_o, g_q, g_k, cmp_pos, cmp_w1, cmp_b1, cmp_w2,
               cache, win_buf, layer, page_table, bias_p, bias_cp, bias_s, bias_c, bias_w):
    flag, gain = _tile_gain([(g_q, N_HEADS), (None, 2 * NSA_KV_HEADS), (g_k[1], NSA_KV_HEADS), (None, NSA_KV_HEADS),
                             (g_k[2], NSA_KV_HEADS), (None, NSA_KV_HEADS)])
    extra = NSA_IN_PAD - flag.shape[1]
    flag = jnp.pad(flag, ((0, 0), (0, extra)))
    gain = jnp.pad(gain, ((0, 0), (0, extra)), constant_values=1.0)
    w = _pad_cols(w_in.astype(BF16), NSA_IN_PAD)
    wo = w_o.astype(BF16)
    zp = _proj(xp, g_mix, w, flag, gain, tm=PROJ_TM, tn=256)
    zs = _proj(xs, g_mix, w, flag, gain, tm=xs.shape[0], tn=256)
    g0 = HD + 6 * KD

    half = CMP_STRIDE * HEAD_DIM
    wcat = jnp.concatenate([cmp_w1[:, :half], cmp_w1[:, half:]], axis=2).astype(BF16)
    pos_flat = jnp.pad(cmp_pos.reshape(2, 1, CMP_LEN * HEAD_DIM), ((0, 0), (0, 7), (0, 0)))
    c0 = _cmp_c0(pos_flat, cmp_w1, cmp_b1.reshape(2, 1, CMP_HIDDEN))
    w2 = cmp_w2.astype(BF16)
    cflag = jnp.stack([jnp.ones((1, HEAD_DIM), F32), jnp.zeros((1, HEAD_DIM), F32)])
    cgain = jnp.stack([g_k[0].astype(F32)[None, :], jnp.ones((1, HEAD_DIM), F32)])

    ident = jnp.arange(nbatch * t // PAGE, dtype=I32).reshape(nbatch, t // PAGE)
    pages_p = zp[:, HD:HD + 4 * KD].reshape(1, nbatch * t // PAGE, PAGE * 4 * NSA_KV_HEADS, HEAD_DIM)
    a_p = _cmp_a(pages_p, 0, ident, wcat)
    cmp_p = _cmp_b(a_p, c0, w2, cflag, cgain)
    gates_p = zp[:, g0:g0 + 3 * N_HEADS].reshape(nbatch * t, NSA_KV_HEADS, 3 * NSA_GROUP).transpose(1, 0, 2)
    op = _nsa_prompt_attn(zp, gates_p, cmp_p, bias_p, bias_cp, nbatch, t)
    xp = _oproj(op, wo, xp, tm=PROJ_TM)

    pages_s = cache.reshape(cache.shape[0], cache.shape[1], PAGE * 4 * NSA_KV_HEADS, HEAD_DIM)
    a_s = _cmp_a(pages_s, layer, page_table, wcat)
    cmp_s = _cmp_b(a_s, c0, w2, cflag, cgain)
    gates_s = zs[:, g0:g0 + 3 * N_HEADS].reshape(sbatch, s_len, N_HEADS, 3).transpose(0, 2, 1, 3)
    gates_s = gates_s.reshape(sbatch, N_HEADS * s_len, 3)
    win_new = zs[:, HD + 4 * KD:g0].reshape(sbatch, s_len, 2 * KD)
    w_all = jnp.concatenate([win_buf.reshape(sbatch, -1, 2 * KD), win_new], axis=1)
    n_w = w_all.shape[1]
    wall = jnp.pad(w_all, ((0, 0), (0, bias_w.shape[1] - n_w), (0, 0)))
    new_rows = jnp.pad(zs[:, HD + 2 * KD:HD + 4 * KD].reshape(sbatch, s_len, 2 * KD), ((0, 0), (0, PAGE - s_len), (0, 0)))
    osm = _nsa_sample_attn(zs, gates_s, cmp_s, bias_c, bias_s, bias_w, wall, new_rows, cache, layer, page_table,
                           s_len, past_len)
    xs = _oproj(osm, wo, xs, tm=xs.shape[0])

    rows_p = zp[:, HD:HD + 4 * KD].reshape(nbatch, t, 4, NSA_KV_HEADS, HEAD_DIM)
    rows_s = zs[:, HD:HD + 4 * KD].reshape(sbatch, s_len, 4, NSA_KV_HEADS, HEAD_DIM)
    keep = min(WINDOW, t)
    win_p = zp[:, HD + 4 * KD:g0].reshape(nbatch, t, 2, NSA_KV_HEADS, HEAD_DIM)[:, t - keep:]
    win_s = w_all[:, s_len:].reshape(sbatch, n_w - s_len, 2, NSA_KV_HEADS, HEAD_DIM)
    return xp, xs, rows_p, rows_s, win_p, win_s


def kernel(x_prompt, x_sample, cache_moba_kv, cache_nsa_kv, state_nsa_win, page_table, rel_bias, ffn_norm, ffn_w_gate, ffn_w_up, ffn_w_down, mix_norm, moba_w_qkv, moba_w_o, moba_q_norm, moba_k_norm, nsa_w_in, nsa_w_o, nsa_q_norm, nsa_k_norm, nsa_cmp_pos, nsa_cmp_w1, nsa_cmp_b1, nsa_cmp_w2):
    nbatch, t, _ = x_prompt.shape
    sbatch, s_len, _ = x_sample.shape
    past_len = page_table.shape[1] * PAGE
    wb = state_nsa_win.shape[2]
    assert t % TQ == 0 and past_len % TQ == 0 and wb == WINDOW and s_len <= 8
    xp = x_prompt.reshape(nbatch * t, D_MODEL)
    xs = x_sample.reshape(sbatch * s_len, D_MODEL)
    rows = N_HEADS * s_len

    bias_p = _bias_tiles(rel_bias, 3, TQ, TQ, 0, TQ)
    bias_s = _bias_tiles(rel_bias, 3, s_len, TQ, 0, TQ).reshape(3, rows, TQ)
    bias_cp = _bias_tiles(rel_bias, t // TQ, TQ, t // CMP_STRIDE, -(CMP_LEN - 1), TQ, CMP_STRIDE)
    n_cmp_s = past_len // CMP_STRIDE
    bias_c = _bias_tiles(rel_bias, 1, s_len, n_cmp_s, past_len - (CMP_LEN - 1), 0, CMP_STRIDE).reshape(rows, n_cmp_s)
    n_w = -(-(wb + s_len) // PAGE) * PAGE
    bias_w = _bias_tiles(rel_bias, 1, s_len, n_w, wb, 0).reshape(rows, n_w)

    moba_p, moba_s, nsa_p, nsa_s, win_p, win_s = [], [], [], [], [], []
    for i in range(DEPTH):
        j = i // 2
        xp, xs = _half_ffn(xp, xs, ffn_norm[i, 0], ffn_w_gate[i, 0], ffn_w_up[i, 0], ffn_w_down[i, 0])
        if i % 2 == 0:
            xp, xs, kv_p, kv_s = _moba_layer(xp, xs, nbatch, t, sbatch, s_len, mix_norm[i], moba_w_qkv[j], moba_w_o[j],
                                             moba_q_norm[j], moba_k_norm[j], cache_moba_kv, j, page_table, bias_p, bias_s)
            moba_p.append(kv_p)
            moba_s.append(kv_s)
        else:
            xp, xs, rp, rs, wp, ws = _nsa_layer(xp, xs, nbatch, t, sbatch, s_len, past_len, mix_norm[i], nsa_w_in[j],
                                                nsa_w_o[j], nsa_q_norm[j], nsa_k_norm[j], nsa_cmp_pos[j], nsa_cmp_w1[j],
                                                nsa_cmp_b1[j], nsa_cmp_w2[j], cache_nsa_kv, state_nsa_win[j], j,
                                                page_table, bias_p, bias_cp, bias_s, bias_c, bias_w)
            nsa_p.append(rp)
            nsa_s.append(rs)
            win_p.append(wp)
            win_s.append(ws)
        xp, xs = _half_ffn(xp, xs, ffn_norm[i, 1], ffn_w_gate[i, 1], ffn_w_up[i, 1], ffn_w_down[i, 1])
    return (xp.reshape(nbatch, t, D_MODEL), xs.reshape(sbatch, s_len, D_MODEL), jnp.stack(moba_p), jnp.stack(moba_s),
            jnp.stack(nsa_p), jnp.stack(nsa_s), jnp.stack(win_p), jnp.stack(win_s))
```
